```python
import jax, jax.numpy as jnp
from jax import lax
import numpy as np

D_MODEL = 2048
BATCH = 4
SEQ = 2048
DEPTH = 4
DEC_BATCH = 8
DEC_SEQ = 4096
PAST_LEN = 128

EPS = 1e-6
BLOCK = 128
D_A = 1024
G_A = 4
DG_A = D_A // G_A
HQ_B = 8
HKV_B = 2
HD_B = 128
D_B = HQ_B * HD_B
DKV_B = HKV_B * HD_B
WINDOW = 128
ROPE_THETA = 10000.0
H_C = 4
DK_C = 512
DV_C = 1024
HDK_C = DK_C // H_C
HDV_C = DV_C // H_C
GATE_RANK = 16
GATE_TEMP = 16.0
CHUNK_C = 64
IN_SPLITS = (D_A, D_A, D_A,
             D_B, DKV_B, DKV_B, D_B,
             DK_C, DK_C, DV_C, DV_C,
             GATE_RANK, GATE_RANK,
             D_MODEL, D_MODEL, D_MODEL)
N_IN = sum(IN_SPLITS)

kernel_name = "hybrid_gmlp_swa_gla_encoder"


def rmsnorm(x, g):
    xf = x.astype(jnp.float32)
    y = xf * lax.rsqrt(jnp.mean(xf * xf, axis=-1, keepdims=True) + EPS) * g.astype(jnp.float32)
    return y.astype(x.dtype)


def rope(x):
    S, hd = x.shape[1], x.shape[-1]
    half = hd // 2
    inv = ROPE_THETA ** (-jnp.arange(half, dtype=jnp.float32) * 2.0 / hd)
    ang = jnp.arange(S, dtype=jnp.float32)[:, None] * inv[None, :]
    cos = jnp.cos(ang)[None, :, None, :].astype(x.dtype)
    sin = jnp.sin(ang)[None, :, None, :].astype(x.dtype)
    x1, x2 = x[..., :half], x[..., half:]
    return jnp.concatenate([x1 * cos - x2 * sin, x2 * cos + x1 * sin], axis=-1)


def spatial_gating(u, v, ln_g, ln_b, ws, bs):
    B, S, _ = v.shape
    n = S // BLOCK
    vf = v.astype(jnp.float32)
    mu = jnp.mean(vf, axis=-1, keepdims=True)
    var = jnp.mean((vf - mu) ** 2, axis=-1, keepdims=True)
    vn = ((vf - mu) * lax.rsqrt(var + EPS) * ln_g + ln_b).astype(v.dtype)
    vn = vn.reshape(B, n, BLOCK, G_A, DG_A)
    f = jnp.einsum('gij,bcjgd->bcigd', ws, vn) + bs.T[None, None, :, :, None]
    return u * f.reshape(B, S, D_A)


def window_attention(q, k, v, sink):
    B, S = q.shape[0], q.shape[1]
    n = S // BLOCK
    W = BLOCK
    G = HQ_B // HKV_B
    q = rope(q)
    k = rope(k)
    qb = q.reshape(B, n, W, HKV_B, G, HD_B)

    def band(t):
        tp = jnp.pad(t, ((0, 0), (W, W), (0, 0), (0, 0))).reshape(B, n + 2, W, HKV_B, HD_B)
        return jnp.concatenate([tp[:, :-2], tp[:, 1:-1], tp[:, 2:]], axis=2)

    kb, vb = band(k), band(v)
    s = jnp.einsum('bnqkgd,bnjkd->bnkgqj', qb, kb).astype(jnp.float32) * (HD_B ** -0.5)
    i = jnp.arange(W)[:, None]
    j = jnp.arange(3 * W)[None, :]
    kpos = jnp.arange(n)[:, None, None] * W - W + j[None]
    mask = (jnp.abs(j - W - i) <= WINDOW)[None] & (kpos >= 0) & (kpos < S)
    s = jnp.where(mask[None, :, None, None], s, -jnp.inf)
    sk = sink.astype(jnp.float32).reshape(HKV_B, G)[None, None, :, :, None, None]
    m = jnp.maximum(jnp.max(s, axis=-1, keepdims=True), sk)
    p = jnp.exp(s - m)
    den = jnp.sum(p, axis=-1, keepdims=True) + jnp.exp(sk - m)
    o = jnp.einsum('bnkgqj,bnjkd->bnqkgd', (p / den).astype(v.dtype), vb)
    return o.reshape(B, S, D_B)


def gla_chunked(q, k, v, g, strict):
    B, S, H, dk = q.shape
    dv = v.shape[-1]
    C = CHUNK_C
    n = S // C
    q = q.reshape(B, n, C, H, dk)
    k = k.reshape(B, n, C, H, dk)
    v = v.reshape(B, n, C, H, dv)
    g = g.reshape(B, n, C, H, dk)
    b = jnp.cumsum(g, axis=2)
    b_last = b[:, :, -1:]
    qe = q * jnp.exp(b)
    ke = k * jnp.exp(-b)
    A = jnp.einsum('bncha,bnmha->bnhcm', qe, ke)
    idx = jnp.arange(C)
    tri = (idx[:, None] > idx[None, :]) if strict else (idx[:, None] >= idx[None, :])
    A = jnp.where(tri, A, 0.0)
    o = jnp.einsum('bnhcm,bnmhv->bnchv', A, v)
    kd = k * jnp.exp(b_last - b)
    dS = jnp.einsum('bnmha,bnmhv->nbhav', kd, v)
    decay = jnp.transpose(jnp.exp(b_last[:, :, 0]), (1, 0, 2, 3))

    def step(state, inp):
        dec, ds = inp
        return dec[..., None] * state + ds, state

    _, s_before = lax.scan(step, jnp.zeros((B, H, dk, dv), jnp.float32), (decay, dS))
    o = o + jnp.einsum('bncha,nbhav->bnchv', qe, s_before)
    return o.reshape(B, S, H, dv)


def bidirectional_gla(q, k, v, lr_f, lr_b, wf, bf, wb, bb, norm_g):
    B, S, _ = q.shape
    f32 = jnp.float32
    qh = q.astype(f32).reshape(B, S, H_C, HDK_C) * (HDK_C ** -0.5)
    kh = k.astype(f32).reshape(B, S, H_C, HDK_C)
    vh = v.astype(f32).reshape(B, S, H_C, HDV_C)
    gf = (jax.nn.log_sigmoid(lr_f.astype(f32) @ wf.astype(f32) + bf.astype(f32)) / GATE_TEMP).reshape(B, S, H_C, HDK_C)
    gb = (jax.nn.log_sigmoid(lr_b.astype(f32) @ wb.astype(f32) + bb.astype(f32)) / GATE_TEMP).reshape(B, S, H_C, HDK_C)
    fwd = gla_chunked(qh, kh, vh, gf, False)
    flip = lambda t: jnp.flip(t, axis=1)
    bwd = flip(gla_chunked(flip(qh), flip(kh), flip(vh), flip(gb), True))
    o = fwd + bwd
    o = o * lax.rsqrt(jnp.mean(o * o, axis=-1, keepdims=True) + EPS) * norm_g.astype(f32)
    return o.reshape(B, S, DV_C).astype(q.dtype)


def hybrid_layer(x, norm_g, w_in, a_ln_g, a_ln_b, a_ws, a_bs, b_sink,
                 c_wf, c_bf, c_wb, c_bb, c_norm_g, w_pa, w_pb, w_pc, w_out):
    B, S, _ = x.shape
    h = rmsnorm(x, norm_g)
    offsets = np.cumsum(IN_SPLITS)[:-1].tolist()
    (ua, va, za, qb, kb, vb, zb, qc, kc, vc, zc, lrf, lrb,
     gate_a, gate_b, gate_c) = [h @ w for w in jnp.split(w_in, offsets, axis=1)]
    ya = spatial_gating(ua, va, a_ln_g, a_ln_b, a_ws, a_bs) * jax.nn.silu(za)
    yb = window_attention(qb.reshape(B, S, HQ_B, HD_B), kb.reshape(B, S, HKV_B, HD_B),
                          vb.reshape(B, S, HKV_B, HD_B), b_sink) * jax.nn.silu(zb)
    yc = bidirectional_gla(qc, kc, vc, lrf, lrb, c_wf, c_bf, c_wb, c_bb, c_norm_g) * jax.nn.silu(zc)
    merged = (jax.nn.sigmoid(gate_a) * (ya @ w_pa)
              + jax.nn.sigmoid(gate_b) * (yb @ w_pb)
              + jax.nn.sigmoid(gate_c) * (yc @ w_pc))
    return x + merged @ w_out


def trunk(x, norm_g, w_in, a_ln_g, a_ln_b, a_ws, a_bs, b_sink,
          c_wf, c_bf, c_wb, c_bb, c_norm_g, w_pa, w_pb, w_pc, w_out, final_g):
    for l in range(DEPTH):
        x = hybrid_layer(x, norm_g[l], w_in[l], a_ln_g[l], a_ln_b[l], a_ws[l], a_bs[l], b_sink[l],
                         c_wf[l], c_bf[l], c_wb[l], c_bb[l], c_norm_g[l],
                         w_pa[l], w_pb[l], w_pc[l], w_out[l])
    return rmsnorm(x, final_g)


def setup_inputs(seed: int = 0) -> dict:
    key = jax.random.key(seed)
    ks = jax.random.split(key, 20)
    f32 = jnp.float32
    nrm = lambda k, shape, s: jax.random.normal(k, shape, f32) * s
    return {
        "x_prompt": nrm(ks[0], (BATCH, SEQ, D_MODEL), 1.0),
        "x_sample": nrm(ks[1], (DEC_BATCH, DEC_SEQ, D_MODEL), 1.0),
        "norm_g": 1.0 + nrm(ks[2], (DEPTH, D_MODEL), 0.05),
        "w_in": nrm(ks[3], (DEPTH, D_MODEL, N_IN), D_MODEL ** -0.5),
        "a_ln_g": 1.0 + nrm(ks[4], (DEPTH, D_A), 0.05),
        "a_ln_b": nrm(ks[5], (DEPTH, D_A), 0.02),
        "a_ws": nrm(ks[6], (DEPTH, G_A, BLOCK, BLOCK), BLOCK ** -0.5),
        "a_bs": 1.0 + nrm(ks[7], (DEPTH, G_A, BLOCK), 0.1),
        "b_sink": nrm(ks[8], (DEPTH, HQ_B), 0.5),
        "c_wf": nrm(ks[9], (DEPTH, GATE_RANK, DK_C), GATE_RANK ** -0.5),
        "c_bf": nrm(ks[10], (DEPTH, DK_C), 0.1),
        "c_wb": nrm(ks[11], (DEPTH, GATE_RANK, DK_C), GATE_RANK ** -0.5),
        "c_bb": nrm(ks[12], (DEPTH, DK_C), 0.1),
        "c_norm_g": 1.0 + nrm(ks[13], (DEPTH, HDV_C), 0.05),
        "w_pa": nrm(ks[14], (DEPTH, D_A, D_MODEL), D_A ** -0.5),
        "w_pb": nrm(ks[15], (DEPTH, D_B, D_MODEL), D_B ** -0.5),
        "w_pc": nrm(ks[16], (DEPTH, DV_C, D_MODEL), DV_C ** -0.5),
        "w_out": nrm(ks[17], (DEPTH, D_MODEL, D_MODEL), D_MODEL ** -0.5),
        "final_g": 1.0 + nrm(ks[18], (D_MODEL,), 0.05),
    }


def reference(x_prompt, x_sample, norm_g, w_in, a_ln_g, a_ln_b, a_ws, a_bs, b_sink,
              c_wf, c_bf, c_wb, c_bb, c_norm_g, w_pa, w_pb, w_pc, w_out, final_g):
    y_prompt = trunk(x_prompt, norm_g, w_in, a_ln_g, a_ln_b, a_ws, a_bs, b_sink,
                     c_wf, c_bf, c_wb, c_bb, c_norm_g, w_pa, w_pb, w_pc, w_out, final_g)
    y_sample = trunk(x_sample, norm_g, w_in, a_ln_g, a_ln_b, a_ws, a_bs, b_sink,
                     c_wf, c_bf, c_wb, c_bb, c_norm_g, w_pa, w_pb, w_pc, w_out, final_g)
    return (y_prompt, y_sample)
```

```python
import functools

import numpy as np
import jax
import jax.numpy as jnp
from jax import lax
from jax.experimental import pallas as pl
from jax.experimental.pallas import tpu as pltpu

F32 = jnp.float32
BF16 = jnp.bfloat16

D_MODEL = 2048
DEPTH = 4
EPS = 1e-6
BLOCK = 128
D_A = 1024
G_A = 4
DG_A = D_A // G_A
HQ_B = 8
HKV_B = 2
HD_B = 128
D_B = HQ_B * HD_B
DKV_B = HKV_B * HD_B
GQ_B = HQ_B // HKV_B
ROPE_THETA = 10000.0
H_C = 4
DK_C = 512
DV_C = 1024
HDK_C = DK_C // H_C
HDV_C = DV_C // H_C
GATE_RANK = 16
GATE_TEMP = 16.0
CHUNK_C = 64
IN_SPLITS = (D_A, D_A, D_A, D_B, DKV_B, DKV_B, D_B, DK_C, DK_C, DV_C, DV_C,
             GATE_RANK, GATE_RANK, D_MODEL, D_MODEL, D_MODEL)

COL_A = 0
COL_QB = 3 * D_A
COL_ZB = COL_QB + D_B
COL_ZC = COL_ZB + D_B
COL_C = COL_ZC + DV_C
COL_GATE = COL_C + 2 * DK_C + DV_C
COL_KVB = COL_GATE + 3 * D_MODEL
COL_LR = COL_KVB + 2 * DKV_B
LANES = 128
N_PROJ = 15360
C_WIDTH = 2 * DK_C + DV_C

PROJ_TM = 1024
PROJ_TN = 1536
GLA_TB = 256
MERGE_TM = 256
NEG_BIG = -1e30
VMEM_LIMIT = 56 * 1024 * 1024


def _silu(z):
    return z * jax.nn.sigmoid(z)


def _rmsnorm_kernel(x_ref, g_ref, o_ref):
    x = x_ref[...]
    y = x * lax.rsqrt(jnp.mean(x * x, axis=-1, keepdims=True) + EPS) * g_ref[...]
    o_ref[...] = y.astype(o_ref.dtype)


def _rmsnorm(x, g, tm=512):
    t = x.shape[0]
    return pl.pallas_call(
        _rmsnorm_kernel,
        grid=(t // tm,),
        in_specs=[pl.BlockSpec((tm, D_MODEL), lambda i: (i, 0)),
                  pl.BlockSpec((1, D_MODEL), lambda i: (0, 0))],
        out_specs=pl.BlockSpec((tm, D_MODEL), lambda i: (i, 0)),
        out_shape=jax.ShapeDtypeStruct((t, D_MODEL), BF16),
        name="rmsnorm0",
    )(x, g)


def _proj_kernel(h_ref, w_ref, o_ref):
    o_ref[...] = jnp.dot(h_ref[...], w_ref[...], preferred_element_type=F32).astype(o_ref.dtype)


def _proj(h, w):
    t = h.shape[0]
    return pl.pallas_call(
        _proj_kernel,
        grid=(t // PROJ_TM, N_PROJ // PROJ_TN),
        in_specs=[pl.BlockSpec((PROJ_TM, D_MODEL), lambda i, j: (i, 0)),
                  pl.BlockSpec((D_MODEL, PROJ_TN), lambda i, j: (0, j))],
        out_specs=pl.BlockSpec((PROJ_TM, PROJ_TN), lambda i, j: (i, j)),
        out_shape=jax.ShapeDtypeStruct((t, N_PROJ), BF16),
        compiler_params=pltpu.CompilerParams(vmem_limit_bytes=VMEM_LIMIT),
        name="proj",
    )(h, w)


def _mix_ab_kernel(sink_ref, a3_ref, q_ref, zb_ref, kvp_ref, kvc_ref, kvn_ref,
                   csp_ref, csc_ref, csn_ref, lng_ref, lnb_ref, ws_ref, bias_ref,
                   ya_ref, yb_ref, *, nb):
    i = pl.program_id(1)

    va = a3_ref[:, D_A:2 * D_A].astype(F32)
    mu = jnp.mean(va, axis=-1, keepdims=True)
    dv = va - mu
    var = jnp.mean(dv * dv, axis=-1, keepdims=True)
    vn = (dv * lax.rsqrt(var + EPS) * lng_ref[...] + lnb_ref[...]).astype(BF16)
    for g in range(G_A):
        cols = slice(g * DG_A, (g + 1) * DG_A)
        f = jnp.dot(ws_ref[g], vn[:, cols], preferred_element_type=F32) + bias_ref[:, cols]
        ua = a3_ref[:, g * DG_A:(g + 1) * DG_A].astype(F32)
        za = a3_ref[:, 2 * D_A + g * DG_A:2 * D_A + (g + 1) * DG_A].astype(F32)
        ya_ref[:, cols] = (ua * f * _silu(za)).astype(ya_ref.dtype)

    def rope(x, cs_ref):
        return x * cs_ref[:, :HD_B] + pltpu.roll(x, HD_B // 2, 1) * cs_ref[:, HD_B:]

    has_prev = (i > 0).astype(jnp.int32)
    has_next = (i < nb - 1).astype(jnp.int32)
    il = lax.broadcasted_iota(jnp.int32, (BLOCK, 3 * BLOCK), 0)
    jc = lax.broadcasted_iota(jnp.int32, (BLOCK, 3 * BLOCK), 1)
    lo = il * has_prev + BLOCK * (1 - has_prev)
    hi = (2 * BLOCK - 1) + (il + 1) * has_next
    mask_bias = jnp.where(jc >= lo, 0.0, NEG_BIG) + jnp.where(jc <= hi, 0.0, NEG_BIG)
    scale = HD_B ** -0.5
    for kh in range(HKV_B):
        kc = slice(kh * HD_B, (kh + 1) * HD_B)
        vc = slice(DKV_B + kh * HD_B, DKV_B + (kh + 1) * HD_B)
        k3 = jnp.concatenate([rope(kvp_ref[:, kc].astype(F32), csp_ref),
                              rope(kvc_ref[:, kc].astype(F32), csc_ref),
                              rope(kvn_ref[:, kc].astype(F32), csn_ref)], axis=0).astype(BF16)
        v3 = jnp.concatenate([kvp_ref[:, vc], kvc_ref[:, vc], kvn_ref[:, vc]], axis=0)
        qs = jnp.concatenate(
            [rope(q_ref[:, (kh * GQ_B + g) * HD_B:(kh * GQ_B + g + 1) * HD_B].astype(F32), csc_ref)
             for g in range(GQ_B)], axis=0).astype(BF16)
        s = lax.dot_general(qs, k3, (((1,), (1,)), ((), ())), preferred_element_type=F32) * scale
        for g in range(GQ_B):
            hq = kh * GQ_B + g
            sink = sink_ref[hq]
            sg = s[g * BLOCK:(g + 1) * BLOCK] + mask_bias
            m = jnp.maximum(jnp.max(sg, axis=-1, keepdims=True), sink)
            p = jnp.exp(sg - m)
            den = jnp.sum(p, axis=-1, keepdims=True) + jnp.exp(sink - m)
            o = jnp.dot(p.astype(BF16), v3, preferred_element_type=F32) / den
            hc = slice(hq * HD_B, (hq + 1) * HD_B)
            yb_ref[:, hc] = (o * _silu(zb_ref[:, hc].astype(F32))).astype(yb_ref.dtype)


def _mix_ab(proj, cs, sink, ln_g, ln_b, ws, bias_full, batch, seq):
    t = batch * seq
    nb = seq // BLOCK

    def row(b, i):
        return b * nb + i

    def prev(i):
        return jnp.maximum(i - 1, 0)

    def nxt(i):
        return jnp.minimum(i + 1, nb - 1)

    kvb = COL_KVB // (2 * DKV_B)
    in_specs = [
        pl.BlockSpec(memory_space=pltpu.SMEM),
        pl.BlockSpec((BLOCK, 3 * D_A), lambda b, i: (row(b, i), COL_A // (3 * D_A))),
        pl.BlockSpec((BLOCK, D_B), lambda b, i: (row(b, i), COL_QB // D_B)),
        pl.BlockSpec((BLOCK, D_B), lambda b, i: (row(b, i), COL_ZB // D_B)),
        pl.BlockSpec((BLOCK, 2 * DKV_B), lambda b, i: (row(b, prev(i)), kvb)),
        pl.BlockSpec((BLOCK, 2 * DKV_B), lambda b, i: (row(b, i), kvb)),
        pl.BlockSpec((BLOCK, 2 * DKV_B), lambda b, i: (row(b, nxt(i)), kvb)),
        pl.BlockSpec((BLOCK, 2 * HD_B), lambda b, i: (prev(i), 0)),
        pl.BlockSpec((BLOCK, 2 * HD_B), lambda b, i: (i, 0)),
        pl.BlockSpec((BLOCK, 2 * HD_B), lambda b, i: (nxt(i), 0)),
        pl.BlockSpec((1, D_A), lambda b, i: (0, 0)),
        pl.BlockSpec((1, D_A), lambda b, i: (0, 0)),
        pl.BlockSpec((G_A, BLOCK, BLOCK), lambda b, i: (0, 0, 0)),
        pl.BlockSpec((BLOCK, D_A), lambda b, i: (0, 0)),
    ]
    out_specs = [pl.BlockSpec((BLOCK, D_A), lambda b, i: (row(b, i), 0)),
                 pl.BlockSpec((BLOCK, D_B), lambda b, i: (row(b, i), 0))]
    return pl.pallas_call(
        functools.partial(_mix_ab_kernel, nb=nb),
        grid=(batch, nb),
        in_specs=in_specs,
        out_specs=out_specs,
        out_shape=[jax.ShapeDtypeStruct((t, D_A), BF16), jax.ShapeDtypeStruct((t, D_B), BF16)],
        name="mix_ab",
    )(sink, proj, proj, proj, proj, proj, proj, cs, cs, cs, ln_g, ln_b, ws, bias_full)


def _gla_direction(c_ref, lr_ref, w_ref, b_ref, o_ref, s_ref, *, reverse):
    n_chunks = GLA_TB // CHUNK_C
    x = jnp.dot(lr_ref[...], w_ref[...], preferred_element_type=F32) + b_ref[...]
    g_all = (jnp.minimum(x, 0.0) - jnp.log1p(jnp.exp(-jnp.abs(x)))) * (1.0 / GATE_TEMP)

    r = lax.broadcasted_iota(jnp.int32, (CHUNK_C, CHUNK_C), 0)
    c = lax.broadcasted_iota(jnp.int32, (CHUNK_C, CHUNK_C), 1)
    if reverse:
        cum = (c >= r)
        keep = c > r
        last = 0
    else:
        cum = (c <= r)
        keep = c <= r
        last = CHUNK_C - 1
    cum = jnp.where(cum, 1.0, 0.0).astype(BF16)

    order = range(n_chunks - 1, -1, -1) if reverse else range(n_chunks)
    for ci in order:
        rows = slice(ci * CHUNK_C, (ci + 1) * CHUNK_C)
        g = g_all[rows]
        g_hi = g.astype(BF16)
        g_lo = (g - g_hi.astype(F32)).astype(BF16)
        b = (jnp.dot(cum, g_hi, preferred_element_type=F32)
             + jnp.dot(cum, g_lo, preferred_element_type=F32))
        b_last = b[last:last + 1]
        q = c_ref[rows, 0:DK_C].astype(F32) * (HDK_C ** -0.5)
        k = c_ref[rows, DK_C:2 * DK_C].astype(F32)
        qe = (q * jnp.exp(b)).astype(BF16)
        ke = (k * jnp.exp(-b)).astype(BF16)
        kd = (k * jnp.exp(b_last - b)).astype(BF16)
        dec = jnp.exp(b_last)
        for h in range(H_C):
            kcols = slice(h * HDK_C, (h + 1) * HDK_C)
            v = c_ref[rows, 2 * DK_C + h * HDV_C:2 * DK_C + (h + 1) * HDV_C]
            a = lax.dot_general(qe[:, kcols], ke[:, kcols], (((1,), (1,)), ((), ())),
                                preferred_element_type=F32)
            a = jnp.where(keep, a, 0.0).astype(BF16)
            st = s_ref[h]
            o = (jnp.dot(a, v, preferred_element_type=F32)
                 + lax.dot_general(qe[:, kcols], st.astype(BF16), (((1,), (1,)), ((), ())),
                                   preferred_element_type=F32))
            ds_t = lax.dot_general(v, kd[:, kcols], (((0,), (0,)), ((), ())),
                                   preferred_element_type=F32)
            s_ref[h] = dec[:, kcols] * st + ds_t
            o_ref[rows, h * HDV_C:(h + 1) * HDV_C] = o.astype(o_ref.dtype)


def _gla_kernel(cf_ref, lrf_ref, cb_ref, lrb_ref, wf_ref, bf_ref, wb_ref, bb_ref,
                of_ref, ob_ref, sf_ref, sb_ref):
    @pl.when(pl.program_id(1) == 0)
    def _():
        sf_ref[...] = jnp.zeros_like(sf_ref)
        sb_ref[...] = jnp.zeros_like(sb_ref)

    _gla_direction(cf_ref, lrf_ref, wf_ref, bf_ref, of_ref, sf_ref, reverse=False)
    _gla_direction(cb_ref, lrb_ref, wb_ref, bb_ref, ob_ref, sb_ref, reverse=True)


def _gla(proj, wf_pad, bf, wb_pad, bb, batch, seq):
    t = batch * seq
    nblk = seq // GLA_TB

    def fwd(b, j):
        return b * nblk + j

    def bwd(b, j):
        return b * nblk + (nblk - 1 - j)

    cblk = COL_C // C_WIDTH
    lrblk = COL_LR // LANES
    in_specs = [
        pl.BlockSpec((GLA_TB, C_WIDTH), lambda b, j: (fwd(b, j), cblk)),
        pl.BlockSpec((GLA_TB, LANES), lambda b, j: (fwd(b, j), lrblk)),
        pl.BlockSpec((GLA_TB, C_WIDTH), lambda b, j: (bwd(b, j), cblk)),
        pl.BlockSpec((GLA_TB, LANES), lambda b, j: (bwd(b, j), lrblk)),
        pl.BlockSpec((LANES, DK_C), lambda b, j: (0, 0)),
        pl.BlockSpec((1, DK_C), lambda b, j: (0, 0)),
        pl.BlockSpec((LANES, DK_C), lambda b, j: (0, 0)),
        pl.BlockSpec((1, DK_C), lambda b, j: (0, 0)),
    ]
    out_specs = [pl.BlockSpec((GLA_TB, DV_C), lambda b, j: (fwd(b, j), 0)),
                 pl.BlockSpec((GLA_TB, DV_C), lambda b, j: (bwd(b, j), 0))]
    return pl.pallas_call(
        _gla_kernel,
        grid=(batch, nblk),
        in_specs=in_specs,
        out_specs=out_specs,
        out_shape=[jax.ShapeDtypeStruct((t, DV_C), BF16), jax.ShapeDtypeStruct((t, DV_C), BF16)],
        scratch_shapes=[pltpu.VMEM((H_C, HDV_C, HDK_C), F32), pltpu.VMEM((H_C, HDV_C, HDK_C), F32)],
        compiler_params=pltpu.CompilerParams(dimension_semantics=("arbitrary", "arbitrary")),
        name="gla",
    )(proj, proj, proj, proj, wf_pad, bf, wb_pad, bb)


def _merge_kernel(ya_ref, yb_ref, of_ref, ob_ref, zc_ref, ga_ref, gb_ref, gc_ref, x_ref,
                  wpa_ref, wpb_ref, wpc_ref, wout_ref, cg_ref, ng_ref, *out_refs, emit_x):
    o = of_ref[...].astype(F32) + ob_ref[...].astype(F32)
    parts = []
    for h in range(H_C):
        oh = o[:, h * HDV_C:(h + 1) * HDV_C]
        ms = jnp.mean(oh * oh, axis=-1, keepdims=True)
        parts.append(oh * lax.rsqrt(ms + EPS) * cg_ref[...])
    yc = (jnp.concatenate(parts, axis=-1) * _silu(zc_ref[...].astype(F32))).astype(BF16)

    def gated(y, w_ref, g_ref):
        return (jax.nn.sigmoid(g_ref[...].astype(F32))
                * jnp.dot(y, w_ref[...], preferred_element_type=F32))

    merged = (gated(ya_ref[...], wpa_ref, ga_ref) + gated(yb_ref[...], wpb_ref, gb_ref)
              + gated(yc, wpc_ref, gc_ref))
    xn = x_ref[...] + jnp.dot(merged.astype(BF16), wout_ref[...], preferred_element_type=F32)
    h_ref = out_refs[-1]
    if emit_x:
        out_refs[0][...] = xn
    hn = xn * lax.rsqrt(jnp.mean(xn * xn, axis=-1, keepdims=True) + EPS) * ng_ref[...]
    h_ref[...] = hn.astype(h_ref.dtype)


def _merge(ya, yb, o_f, o_b, proj, x, w_pa, w_pb, w_pc, w_out, c_norm_g, next_g, last):
    t = x.shape[0]
    tm = MERGE_TM
    tok = lambda width, blk: pl.BlockSpec((tm, width), lambda i: (i, blk))
    const = lambda shape: pl.BlockSpec(shape, lambda i: (0,) * len(shape),
                                       pipeline_mode=pl.Buffered(1))
    gate0 = COL_GATE // D_MODEL
    in_specs = [
        tok(D_A, 0), tok(D_B, 0), tok(DV_C, 0), tok(DV_C, 0),
        tok(DV_C, COL_ZC // DV_C),
        tok(D_MODEL, gate0), tok(D_MODEL, gate0 + 1), tok(D_MODEL, gate0 + 2),
        tok(D_MODEL, 0),
        const((D_A, D_MODEL)), const((D_B, D_MODEL)), const((DV_C, D_MODEL)),
        const((D_MODEL, D_MODEL)), const((1, HDV_C)), const((1, D_MODEL)),
    ]
    h_dtype = F32 if last else BF16
    out_specs = [tok(D_MODEL, 0)]
    out_shape = [jax.ShapeDtypeStruct((t, D_MODEL), h_dtype)]
    if not last:
        out_specs = [tok(D_MODEL, 0)] + out_specs
        out_shape = [jax.ShapeDtypeStruct((t, D_MODEL), F32)] + out_shape
    outs = pl.pallas_call(
        functools.partial(_merge_kernel, emit_x=not last),
        grid=(t // tm,),
        in_specs=in_specs,
        out_specs=out_specs,
        out_shape=out_shape,
        compiler_params=pltpu.CompilerParams(vmem_limit_bytes=VMEM_LIMIT),
        name="merge",
    )(ya, yb, o_f, o_b, proj, proj, proj, proj, x, w_pa, w_pb, w_pc, w_out, c_norm_g, next_g)
    return (None, outs[0]) if last else (outs[0], outs[1])


def _reorder_w_in(w_in):
    offs = np.concatenate([[0], np.cumsum(IN_SPLITS)])
    seg = lambda i: w_in[:, :, offs[i]:offs[i + 1]]
    (ua, va, za, qb, kb, vb, zb, qc, kc, vc, zc, lrf, lrb, ga, gb, gc) = [seg(i) for i in range(16)]
    pad = jnp.zeros(w_in.shape[:2] + (N_PROJ - COL_LR - 2 * GATE_RANK,), w_in.dtype)
    cols = [ua, va, za, qb, zb, zc, qc, kc, vc, ga, gb, gc, kb, vb, lrf, lrb, pad]
    return jnp.concatenate(cols, axis=-1).astype(BF16)


def _rope_table(seq):
    half = HD_B // 2
    inv = ROPE_THETA ** (-jnp.arange(half, dtype=F32) * 2.0 / HD_B)
    ang = jnp.arange(seq, dtype=F32)[:, None] * inv[None, :]
    cos, sin = jnp.cos(ang), jnp.sin(ang)
    return jnp.concatenate([cos, cos, -sin, sin], axis=-1)


def _pad_gate_w(w, row0):
    z = jnp.zeros((w.shape[0], LANES, DK_C), w.dtype)
    return z.at[:, row0:row0 + GATE_RANK].set(w).astype(BF16)


def _trunk(x, params, batch, seq):
    t = batch * seq
    x = x.reshape(t, D_MODEL)
    cs = _rope_table(seq)
    h = _rmsnorm(x, params["norm_g"][0][None])
    for l in range(DEPTH):
        last = l == DEPTH - 1
        proj = _proj(h, params["w_in"][l])
        ya, yb = _mix_ab(proj, cs, params["b_sink"][l], params["a_ln_g"][l][None],
                         params["a_ln_b"][l][None], params["a_ws"][l], params["a_bias"][l],
                         batch, seq)
        o_f, o_b = _gla(proj, params["c_wf"][l], params["c_bf"][l][None], params["c_wb"][l],
                        params["c_bb"][l][None], batch, seq)
        next_g = params["final_g"] if last else params["norm_g"][l + 1]
        x, h = _merge(ya, yb, o_f, o_b, proj, x, params["w_pa"][l], params["w_pb"][l],
                      params["w_pc"][l], params["w_out"][l], params["c_norm_g"][l][None],
                      next_g[None], last)
    return h.reshape(batch, seq, D_MODEL)


def kernel(x_prompt, x_sample, norm_g, w_in, a_ln_g, a_ln_b, a_ws, a_bs, b_sink, c_wf, c_bf,
           c_wb, c_bb, c_norm_g, w_pa, w_pb, w_pc, w_out, final_g):
    params = dict(
        norm_g=norm_g, final_g=final_g, w_in=_reorder_w_in(w_in),
        a_ln_g=a_ln_g, a_ln_b=a_ln_b, a_ws=a_ws.astype(BF16),
        a_bias=jnp.repeat(jnp.swapaxes(a_bs, 1, 2), DG_A, axis=2),
        b_sink=b_sink,
        c_wf=_pad_gate_w(c_wf, 0), c_bf=c_bf, c_wb=_pad_gate_w(c_wb, GATE_RANK), c_bb=c_bb,
        c_norm_g=c_norm_g, w_pa=w_pa.astype(BF16), w_pb=w_pb.astype(BF16),
        w_pc=w_pc.astype(BF16), w_out=w_out.astype(BF16),
    )
    y_prompt = _trunk(x_prompt, params, x_prompt.shape[0], x_prompt.shape[1])
    y_sample = _trunk(x_sample, params, x_sample.shape[0], x_sample.shape[1])
    return (y_prompt, y_sample)
```

```python
import functools

import numpy as np
import jax
import jax.numpy as jnp
from jax import lax
from jax.experimental import pallas as pl
from jax.experimental.pallas import tpu as pltpu

F32 = jnp.float32
BF16 = jnp.bfloat16

D_MODEL = 2048
DEPTH = 4
EPS = 1e-6
BLOCK = 128
D_A = 1024
G_A = 4
DG_A = D_A // G_A
HQ_B = 8
HKV_B = 2
HD_B = 128
D_B = HQ_B * HD_B
DKV_B = HKV_B * HD_B
GQ_B = HQ_B // HKV_B
ROPE_THETA = 10000.0
H_C = 4
DK_C = 512
DV_C = 1024
HDK_C = DK_C // H_C
HDV_C = DV_C // H_C
GATE_RANK = 16
GATE_TEMP = 16.0
CHUNK_C = 64
IN_SPLITS = (D_A, D_A, D_A, D_B, DKV_B, DKV_B, D_B, DK_C, DK_C, DV_C, DV_C,
             GATE_RANK, GATE_RANK, D_MODEL, D_MODEL, D_MODEL)
N_IN = sum(IN_SPLITS)
SEG_ORDER = (0, 1, 2, 3, 6, 10, 7, 8, 9, 13, 14, 15, 4, 5, 11, 12)

COL_A = 0
COL_QB = 3 * D_A
COL_ZB = COL_QB + D_B
COL_ZC = COL_ZB + D_B
COL_C = COL_ZC + DV_C
COL_GATE = COL_C + 2 * DK_C + DV_C
COL_KVB = COL_GATE + 3 * D_MODEL
COL_LR = COL_KVB + 2 * DKV_B
LANES = 128
N_PROJ = 15360
C_WIDTH = 2 * DK_C + DV_C

PREP_TR = 128
PROJ_TM = 1024
PROJ_TN = 1536
GLA_TB = 256
MM_TM = 2 * BLOCK
MERGE_TN = 512
NEG_BIG = -1e30
LOG2E = 1.4426950408889634
VMEM_LIMIT = 56 * 1024 * 1024


def _silu(z):
    return z * jax.nn.sigmoid(z)


def _prep_segments():
    offs = np.concatenate([[0], np.cumsum(IN_SPLITS)])
    segs, dst = [], 0
    for i in SEG_ORDER:
        segs.append((dst, int(offs[i]), IN_SPLITS[i]))
        dst += IN_SPLITS[i]
    return segs, dst


def _prep_w_in_kernel(w_ref, o_ref):
    segs, pad0 = _prep_segments()
    for dst, src, width in segs:
        o_ref[:, dst:dst + width] = w_ref[:, src:src + width].astype(o_ref.dtype)
    o_ref[:, pad0:] = jnp.zeros((o_ref.shape[0], N_PROJ - pad0), o_ref.dtype)


def _prep_w_in(w_in):
    depth = w_in.shape[0]
    return pl.pallas_call(
        _prep_w_in_kernel,
        grid=(depth, D_MODEL // PREP_TR),
        in_specs=[pl.BlockSpec((None, PREP_TR, N_IN), lambda l, i: (l, i, 0))],
        out_specs=pl.BlockSpec((None, PREP_TR, N_PROJ), lambda l, i: (l, i, 0)),
        out_shape=jax.ShapeDtypeStruct((depth, D_MODEL, N_PROJ), BF16),
        compiler_params=pltpu.CompilerParams(vmem_limit_bytes=VMEM_LIMIT),
        name="prep_w_in",
    )(w_in)


def _rmsnorm_kernel(x_ref, g_ref, o_ref):
    x = x_ref[...]
    y = x * lax.rsqrt(jnp.mean(x * x, axis=-1, keepdims=True) + EPS) * g_ref[...]
    o_ref[...] = y.astype(o_ref.dtype)


def _rmsnorm(x, g, tm=512):
    t = x.shape[0]
    return pl.pallas_call(
        _rmsnorm_kernel,
        grid=(t // tm,),
        in_specs=[pl.BlockSpec((tm, D_MODEL), lambda i: (i, 0)),
                  pl.BlockSpec((1, D_MODEL), lambda i: (0, 0))],
        out_specs=pl.BlockSpec((tm, D_MODEL), lambda i: (i, 0)),
        out_shape=jax.ShapeDtypeStruct((t, D_MODEL), BF16),
        name="rmsnorm0",
    )(x, g)


def _proj_kernel(h_ref, w_ref, o_ref):
    o_ref[...] = jnp.dot(h_ref[...], w_ref[...], preferred_element_type=F32).astype(o_ref.dtype)


def _proj(h, w_all, layer):
    t = h.shape[0]
    return pl.pallas_call(
        _proj_kernel,
        grid=(t // PROJ_TM, N_PROJ // PROJ_TN),
        in_specs=[pl.BlockSpec((PROJ_TM, D_MODEL), lambda i, j: (i, 0)),
                  pl.BlockSpec((None, D_MODEL, PROJ_TN), lambda i, j: (layer, 0, j))],
        out_specs=pl.BlockSpec((PROJ_TM, PROJ_TN), lambda i, j: (i, j)),
        out_shape=jax.ShapeDtypeStruct((t, N_PROJ), BF16),
        compiler_params=pltpu.CompilerParams(vmem_limit_bytes=VMEM_LIMIT),
        name="proj",
    )(h, w_all)


def _gla_direction(c_ref, lr_ref, w_ref, b_ref, o_ref, s_ref, *, reverse):
    n_chunks = GLA_TB // CHUNK_C
    x = jnp.dot(lr_ref[...], w_ref[...], preferred_element_type=F32) + b_ref[...]
    g_all = (jnp.minimum(x, 0.0) - jnp.log1p(jnp.exp(-jnp.abs(x)))) * (1.0 / GATE_TEMP)

    r = lax.broadcasted_iota(jnp.int32, (CHUNK_C, CHUNK_C), 0)
    c = lax.broadcasted_iota(jnp.int32, (CHUNK_C, CHUNK_C), 1)
    if reverse:
        cum = (c >= r)
        keep = c > r
        last = 0
    else:
        cum = (c <= r)
        keep = c <= r
        last = CHUNK_C - 1
    cum = jnp.where(cum, 1.0, 0.0).astype(BF16)

    order = range(n_chunks - 1, -1, -1) if reverse else range(n_chunks)
    for ci in order:
        rows = slice(ci * CHUNK_C, (ci + 1) * CHUNK_C)
        g = g_all[rows]
        g_hi = g.astype(BF16)
        g_lo = (g - g_hi.astype(F32)).astype(BF16)
        b = (jnp.dot(cum, g_hi, preferred_element_type=F32)
             + jnp.dot(cum, g_lo, preferred_element_type=F32))
        b_last = b[last:last + 1]
        q = c_ref[rows, 0:DK_C].astype(F32) * (HDK_C ** -0.5)
        k = c_ref[rows, DK_C:2 * DK_C].astype(F32)
        qe = (q * jnp.exp(b)).astype(BF16)
        ke = (k * jnp.exp(-b)).astype(BF16)
        kd = (k * jnp.exp(b_last - b)).astype(BF16)
        dec = jnp.exp(b_last)
        for h in range(H_C):
            kcols = slice(h * HDK_C, (h + 1) * HDK_C)
            v = c_ref[rows, 2 * DK_C + h * HDV_C:2 * DK_C + (h + 1) * HDV_C]
            a = lax.dot_general(qe[:, kcols], ke[:, kcols], (((1,), (1,)), ((), ())),
                                preferred_element_type=F32)
            a = jnp.where(keep, a, 0.0).astype(BF16)
            st = s_ref[h]
            o = (jnp.dot(a, v, preferred_element_type=F32)
                 + lax.dot_general(qe[:, kcols], st.astype(BF16), (((1,), (1,)), ((), ())),
                                   preferred_element_type=F32))
            ds_t = lax.dot_general(v, kd[:, kcols], (((0,), (0,)), ((), ())),
                                   preferred_element_type=F32)
            s_ref[h] = dec[:, kcols] * st + ds_t
            o_ref[rows, h * HDV_C:(h + 1) * HDV_C] = o.astype(o_ref.dtype)


def _gla_kernel(cf_ref, lrf_ref, cb_ref, lrb_ref, wf_ref, bf_ref, wb_ref, bb_ref,
                of_ref, ob_ref, sf_ref, sb_ref):
    @pl.when(pl.program_id(1) == 0)
    def _():
        sf_ref[...] = jnp.zeros_like(sf_ref)
        sb_ref[...] = jnp.zeros_like(sb_ref)

    _gla_direction(cf_ref, lrf_ref, wf_ref, bf_ref, of_ref, sf_ref, reverse=False)
    _gla_direction(cb_ref, lrb_ref, wb_ref, bb_ref, ob_ref, sb_ref, reverse=True)


def _gla(proj, wf_pad, bf, wb_pad, bb, batch, seq):
    t = batch * seq
    nblk = seq // GLA_TB

    def fwd(b, j):
        return b * nblk + j

    def bwd(b, j):
        return b * nblk + (nblk - 1 - j)

    cblk = COL_C // C_WIDTH
    lrblk = COL_LR // LANES
    in_specs = [
        pl.BlockSpec((GLA_TB, C_WIDTH), lambda b, j: (fwd(b, j), cblk)),
        pl.BlockSpec((GLA_TB, LANES), lambda b, j: (fwd(b, j), lrblk)),
        pl.BlockSpec((GLA_TB, C_WIDTH), lambda b, j: (bwd(b, j), cblk)),
        pl.BlockSpec((GLA_TB, LANES), lambda b, j: (bwd(b, j), lrblk)),
        pl.BlockSpec((LANES, DK_C), lambda b, j: (0, 0)),
        pl.BlockSpec((1, DK_C), lambda b, j: (0, 0)),
        pl.BlockSpec((LANES, DK_C), lambda b, j: (0, 0)),
        pl.BlockSpec((1, DK_C), lambda b, j: (0, 0)),
    ]
    out_specs = [pl.BlockSpec((GLA_TB, DV_C), lambda b, j: (fwd(b, j), 0)),
                 pl.BlockSpec((GLA_TB, DV_C), lambda b, j: (bwd(b, j), 0))]
    return pl.pallas_call(
        _gla_kernel,
        grid=(batch, nblk),
        in_specs=in_specs,
        out_specs=out_specs,
        out_shape=[jax.ShapeDtypeStruct((t, DV_C), BF16), jax.ShapeDtypeStruct((t, DV_C), BF16)],
        scratch_shapes=[pltpu.VMEM((H_C, HDV_C, HDK_C), F32), pltpu.VMEM((H_C, HDV_C, HDK_C), F32)],
        compiler_params=pltpu.CompilerParams(dimension_semantics=("arbitrary", "arbitrary")),
        name="gla",
    )(proj, proj, proj, proj, wf_pad, bf, wb_pad, bb)


def _mix_a_items(a3_ref, lng_ref, lnb_ref, ws_ref, bias_ref, ya_ref):
    def item(blk):
        rows = slice(blk * BLOCK, (blk + 1) * BLOCK)
        va = a3_ref[rows, D_A:2 * D_A].astype(F32)
        mu = jnp.mean(va, axis=-1, keepdims=True)
        dv = va - mu
        var = jnp.mean(dv * dv, axis=-1, keepdims=True)
        vn = (dv * lax.rsqrt(var + EPS) * lng_ref[...] + lnb_ref[...]).astype(BF16)
        for g in range(G_A):
            cols = slice(g * DG_A, (g + 1) * DG_A)
            f = jnp.dot(ws_ref[g], vn[:, cols], preferred_element_type=F32) + bias_ref[:, cols]
            ua = a3_ref[rows, g * DG_A:(g + 1) * DG_A].astype(F32)
            za = a3_ref[rows, 2 * D_A + g * DG_A:2 * D_A + (g + 1) * DG_A].astype(F32)
            ya_ref[rows, cols] = (ua * f * _silu(za)).astype(ya_ref.dtype)

    return [functools.partial(item, blk) for blk in range(MM_TM // BLOCK)]


def _mix_b_items(sink_ref, q_ref, zb_ref, kvp_ref, kvc_ref, kvn_ref, csp_ref, csc_ref, csn_ref,
                 yb_ref, first, final):
    def rope(x, cs):
        return x * cs[:, :HD_B] + pltpu.roll(x, HD_B // 2, 1) * cs[:, HD_B:]

    q_scale = (HD_B ** -0.5) * LOG2E
    roped = {}

    def item(kh, blk):
        kcol = slice(kh * HD_B, (kh + 1) * HD_B)
        vcol = slice(DKV_B + kh * HD_B, DKV_B + (kh + 1) * HD_B)
        if kh not in roped:
            cs_k = jnp.concatenate([csp_ref[...], csc_ref[...], csn_ref[...]], axis=0)
            k_raw = jnp.concatenate([kvp_ref[:, kcol], kvc_ref[:, kcol], kvn_ref[:, kcol]], axis=0)
            k4 = rope(k_raw.astype(F32), cs_k).astype(BF16)
            v4 = jnp.concatenate([kvp_ref[:, vcol], kvc_ref[:, vcol], kvn_ref[:, vcol]], axis=0)
            roped[kh] = (k4, v4)
        k4, v4 = roped[kh]
        rows = slice(blk * BLOCK, (blk + 1) * BLOCK)
        il = lax.broadcasted_iota(jnp.int32, (BLOCK, 3 * BLOCK), 0)
        jc = lax.broadcasted_iota(jnp.int32, (BLOCK, 3 * BLOCK), 1)
        has_prev = 1 if blk > 0 else 1 - first
        has_next = 1 if blk < MM_TM // BLOCK - 1 else 1 - final
        lo = il * has_prev + BLOCK * (1 - has_prev)
        hi = (2 * BLOCK - 1) + (il + 1) * has_next
        mask_bias = jnp.where(jc >= lo, 0.0, NEG_BIG) + jnp.where(jc <= hi, 0.0, NEG_BIG)
        k3 = k4[blk * BLOCK:(blk + 3) * BLOCK]
        v3 = v4[blk * BLOCK:(blk + 3) * BLOCK]
        cs_q = csc_ref[rows, :]
        qs = jnp.concatenate(
            [rope(q_ref[rows, (kh * GQ_B + g) * HD_B:(kh * GQ_B + g + 1) * HD_B].astype(F32),
                  cs_q) * q_scale for g in range(GQ_B)], axis=0).astype(BF16)
        s = lax.dot_general(qs, k3, (((1,), (1,)), ((), ())), preferred_element_type=F32)
        for g in range(GQ_B):
            hq = kh * GQ_B + g
            sink = sink_ref[hq] * LOG2E
            sg = s[g * BLOCK:(g + 1) * BLOCK] + mask_bias
            m = jnp.maximum(jnp.max(sg, axis=-1, keepdims=True), sink)
            p = jnp.exp2(sg - m)
            den = jnp.sum(p, axis=-1, keepdims=True) + jnp.exp2(sink - m)
            o = jnp.dot(p.astype(BF16), v3, preferred_element_type=F32) / den
            hc = slice(hq * HD_B, (hq + 1) * HD_B)
            yb_ref[rows, hc] = (o * _silu(zb_ref[rows, hc].astype(F32))).astype(yb_ref.dtype)

    return [functools.partial(item, kh, blk)
            for kh in range(HKV_B) for blk in range(MM_TM // BLOCK)]


def _mix_c_items(of_ref, ob_ref, zc_ref, cg_ref, yc_ref):
    def item(h):
        hc = slice(h * HDV_C, (h + 1) * HDV_C)
        oh = of_ref[:, hc].astype(F32) + ob_ref[:, hc].astype(F32)
        ms = jnp.mean(oh * oh, axis=-1, keepdims=True)
        yc_ref[:, hc] = (oh * lax.rsqrt(ms + EPS) * cg_ref[...]
                         * _silu(zc_ref[:, hc].astype(F32))).astype(yc_ref.dtype)

    return [functools.partial(item, h) for h in range(H_C)]


def _merge_items(ya_ref, yb_ref, yc_ref, ga_ref, gb_ref, gc_ref, x_ref,
                 wpa_ref, wpb_ref, wpc_ref, wout_ref, ng_ref, x_out_ref, h_ref,
                 mg_s, xn_s, ss_s):
    n_chunks = D_MODEL // MERGE_TN

    def branch(c):
        cols = slice(c * MERGE_TN, (c + 1) * MERGE_TN)

        def gated(y_ref, w_ref, g_ref):
            return (jax.nn.sigmoid(g_ref[:, cols].astype(F32))
                    * jnp.dot(y_ref[...], w_ref[:, cols], preferred_element_type=F32))

        mg_s[:, cols] = (gated(ya_ref, wpa_ref, ga_ref) + gated(yb_ref, wpb_ref, gb_ref)
                         + gated(yc_ref, wpc_ref, gc_ref)).astype(mg_s.dtype)

    def out(c):
        cols = slice(c * MERGE_TN, (c + 1) * MERGE_TN)
        xn = x_ref[:, cols] + jnp.dot(mg_s[...], wout_ref[:, cols], preferred_element_type=F32)
        xn_s[:, cols] = xn
        if x_out_ref is not None:
            x_out_ref[:, cols] = xn
        part = jnp.sum(xn * xn, axis=-1, keepdims=True)
        ss_s[...] = part if c == 0 else ss_s[...] + part

    def norm():
        hn = xn_s[...] * lax.rsqrt(ss_s[...] * (1.0 / D_MODEL) + EPS) * ng_ref[...]
        h_ref[...] = hn.astype(h_ref.dtype)

    return ([functools.partial(branch, c) for c in range(n_chunks)]
            + [functools.partial(out, c) for c in range(n_chunks)] + [norm])


def _mixmerge_kernel(sink_ref, a3_ref, q_ref, zb_ref, kvp_ref, kvc_ref, kvn_ref,
                     csp_ref, csc_ref, csn_ref, lng_ref, lnb_ref, ws_ref, bias_ref,
                     of_ref, ob_ref, zc_ref, ga_ref, gb_ref, gc_ref, x_ref,
                     wpa_ref, wpb_ref, wpc_ref, wout_ref, cg_ref, ng_ref,
                     *rest, n_tiles, tiles_per_seq, emit_x):
    if emit_x:
        x_out_ref, h_ref, ya_s, yb_s, yc_s, mg_s, xn_s, ss_s = rest
    else:
        x_out_ref = None
        h_ref, ya_s, yb_s, yc_s, mg_s, xn_s, ss_s = rest
    r = pl.program_id(0)
    slot = lax.rem(r, 2)

    @pl.when(r == 0)
    def _():
        ya_s[1] = jnp.zeros(ya_s.shape[1:], ya_s.dtype)
        yb_s[1] = jnp.zeros(yb_s.shape[1:], yb_s.dtype)
        yc_s[1] = jnp.zeros(yc_s.shape[1:], yc_s.dtype)

    pos = lax.rem(jnp.minimum(r, n_tiles - 1), tiles_per_seq)
    first = (pos == 0).astype(jnp.int32)
    final = (pos == tiles_per_seq - 1).astype(jnp.int32)
    mix_a = _mix_a_items(a3_ref, lng_ref, lnb_ref, ws_ref, bias_ref, ya_s.at[slot])
    mix_b = _mix_b_items(sink_ref, q_ref, zb_ref, kvp_ref, kvc_ref, kvn_ref, csp_ref, csc_ref,
                         csn_ref, yb_s.at[slot], first, final)
    mix_c = _mix_c_items(of_ref, ob_ref, zc_ref, cg_ref, yc_s.at[slot])
    merge = _merge_items(ya_s.at[1 - slot], yb_s.at[1 - slot], yc_s.at[1 - slot], ga_ref, gb_ref,
                         gc_ref, x_ref, wpa_ref, wpb_ref, wpc_ref, wout_ref, ng_ref, x_out_ref,
                         h_ref, mg_s, xn_s, ss_s)
    mix = [[mix_c[0], mix_c[1]], [mix_a[0]], [mix_b[0]], [mix_b[1]], [mix_c[2], mix_c[3]],
           [mix_b[2]], [mix_b[3]], [mix_a[1]], []]
    for merge_item, mix_group in zip(merge, mix):
        merge_item()
        for mix_item in mix_group:
            mix_item()


def _mixmerge(proj, o_f, o_b, x, cs, params, layer, next_g, batch, seq, last):
    t = batch * seq
    n_tiles = t // MM_TM
    tps = seq // MM_TM
    nb2 = MM_TM // BLOCK

    def mix_tile(r):
        return jnp.minimum(r, n_tiles - 1)

    def merge_tile(r):
        return jnp.maximum(r - 1, 0)

    def kv_prev(r):
        m = mix_tile(r)
        return jnp.where(lax.rem(m, tps) == 0, nb2 * m, nb2 * m - 1)

    def kv_next(r):
        m = mix_tile(r)
        return jnp.where(lax.rem(m, tps) == tps - 1, nb2 * m + nb2 - 1, nb2 * m + nb2)

    def cs_prev(r):
        i = lax.rem(mix_tile(r), tps)
        return jnp.maximum(nb2 * i - 1, 0)

    def cs_next(r):
        i = lax.rem(mix_tile(r), tps)
        return jnp.minimum(nb2 * i + nb2, nb2 * tps - 1)

    mix = lambda width, blk: pl.BlockSpec((MM_TM, width), lambda r: (mix_tile(r), blk))
    mrg = lambda width, blk: pl.BlockSpec((MM_TM, width), lambda r: (merge_tile(r), blk))
    per_layer = lambda shape: pl.BlockSpec((None,) + shape, lambda r: (layer,) + (0,) * len(shape))
    weight = lambda shape: pl.BlockSpec((None,) + shape, lambda r: (layer,) + (0,) * len(shape),
                                        pipeline_mode=pl.Buffered(1))
    kvb = COL_KVB // (2 * DKV_B)
    gate0 = COL_GATE // D_MODEL
    in_specs = [
        pl.BlockSpec(memory_space=pltpu.SMEM),
        mix(3 * D_A, COL_A // (3 * D_A)), mix(D_B, COL_QB // D_B), mix(D_B, COL_ZB // D_B),
        pl.BlockSpec((BLOCK, 2 * DKV_B), lambda r: (kv_prev(r), kvb)),
        mix(2 * DKV_B, kvb),
        pl.BlockSpec((BLOCK, 2 * DKV_B), lambda r: (kv_next(r), kvb)),
        pl.BlockSpec((BLOCK, 2 * HD_B), lambda r: (cs_prev(r), 0)),
        pl.BlockSpec((MM_TM, 2 * HD_B), lambda r: (lax.rem(mix_tile(r), tps), 0)),
        pl.BlockSpec((BLOCK, 2 * HD_B), lambda r: (cs_next(r), 0)),
        per_layer((1, D_A)), per_layer((1, D_A)), per_layer((G_A, BLOCK, BLOCK)),
        per_layer((BLOCK, D_A)),
        mix(DV_C, 0), mix(DV_C, 0), mix(DV_C, COL_ZC // DV_C),
        mrg(D_MODEL, gate0), mrg(D_MODEL, gate0 + 1), mrg(D_MODEL, gate0 + 2),
        mrg(D_MODEL, 0),
        weight((D_A, D_MODEL)), weight((D_B, D_MODEL)), weight((DV_C, D_MODEL)),
        weight((D_MODEL, D_MODEL)), per_layer((1, HDV_C)),
        pl.BlockSpec((1, D_MODEL), lambda r: (0, 0)),
    ]
    h_dtype = F32 if last else BF16
    out_specs = [mrg(D_MODEL, 0)]
    out_shape = [jax.ShapeDtypeStruct((t, D_MODEL), h_dtype)]
    if not last:
        out_specs = [mrg(D_MODEL, 0)] + out_specs
        out_shape = [jax.ShapeDtypeStruct((t, D_MODEL), F32)] + out_shape
    outs = pl.pallas_call(
        functools.partial(_mixmerge_kernel, n_tiles=n_tiles, tiles_per_seq=tps, emit_x=not last),
        grid=(n_tiles + 1,),
        in_specs=in_specs,
        out_specs=out_specs,
        out_shape=out_shape,
        scratch_shapes=[pltpu.VMEM((2, MM_TM, D_A), BF16), pltpu.VMEM((2, MM_TM, D_B), BF16),
                        pltpu.VMEM((2, MM_TM, DV_C), BF16), pltpu.VMEM((MM_TM, D_MODEL), BF16),
                        pltpu.VMEM((MM_TM, D_MODEL), F32), pltpu.VMEM((MM_TM, 1), F32)],
        compiler_params=pltpu.CompilerParams(dimension_semantics=("arbitrary",),
                                             vmem_limit_bytes=VMEM_LIMIT),
        name="mixmerge",
    )(params["b_sink"][layer], proj, proj, proj, proj, proj, proj, cs, cs, cs,
      params["a_ln_g"], params["a_ln_b"], params["a_ws"], params["a_bias"],
      o_f, o_b, proj, proj, proj, proj, x,
      params["w_pa"], params["w_pb"], params["w_pc"], params["w_out"], params["c_norm_g"], next_g)
    return (None, outs[0]) if last else (outs[0], outs[1])


def _rope_table(seq):
    half = HD_B // 2
    inv = ROPE_THETA ** (-jnp.arange(half, dtype=F32) * 2.0 / HD_B)
    ang = jnp.arange(seq, dtype=F32)[:, None] * inv[None, :]
    cos, sin = jnp.cos(ang), jnp.sin(ang)
    return jnp.concatenate([cos, cos, -sin, sin], axis=-1)


def _pad_gate_w(w, row0):
    z = jnp.zeros((w.shape[0], LANES, DK_C), w.dtype)
    return z.at[:, row0:row0 + GATE_RANK].set(w).astype(BF16)


def _prepare(norm_g, w_in, a_ln_g, a_ln_b, a_ws, a_bs, b_sink, c_wf, c_bf, c_wb, c_bb, c_norm_g,
             w_pa, w_pb, w_pc, w_out, final_g):
    return dict(
        norm_g=norm_g, final_g=final_g, w_in=_prep_w_in(w_in),
        a_ln_g=a_ln_g[:, None], a_ln_b=a_ln_b[:, None], a_ws=a_ws.astype(BF16),
        a_bias=jnp.repeat(jnp.swapaxes(a_bs, 1, 2), DG_A, axis=2),
        b_sink=b_sink,
        c_wf=_pad_gate_w(c_wf, 0), c_bf=c_bf, c_wb=_pad_gate_w(c_wb, GATE_RANK), c_bb=c_bb,
        c_norm_g=c_norm_g[:, None], w_pa=w_pa.astype(BF16), w_pb=w_pb.astype(BF16),
        w_pc=w_pc.astype(BF16), w_out=w_out.astype(BF16),
    )


def _trunk(x, params, batch, seq):
    t = batch * seq
    x = x.reshape(t, D_MODEL)
    cs = _rope_table(seq)
    h = _rmsnorm(x, params["norm_g"][0][None])
    for l in range(DEPTH):
        last = l == DEPTH - 1
        proj = _proj(h, params["w_in"], l)
        o_f, o_b = _gla(proj, params["c_wf"][l], params["c_bf"][l][None], params["c_wb"][l],
                        params["c_bb"][l][None], batch, seq)
        next_g = params["final_g"] if last else params["norm_g"][l + 1]
        x, h = _mixmerge(proj, o_f, o_b, x, cs, params, l, next_g[None], batch, seq, last)
    return h.reshape(batch, seq, D_MODEL)


def kernel(x_prompt, x_sample, norm_g, w_in, a_ln_g, a_ln_b, a_ws, a_bs, b_sink, c_wf, c_bf,
           c_wb, c_bb, c_norm_g, w_pa, w_pb, w_pc, w_out, final_g):
    params = _prepare(norm_g, w_in, a_ln_g, a_ln_b, a_ws, a_bs, b_sink, c_wf, c_bf, c_wb, c_bb,
                      c_norm_g, w_pa, w_pb, w_pc, w_out, final_g)
    y_prompt = _trunk(x_prompt, params, x_prompt.shape[0], x_prompt.shape[1])
    y_sample = _trunk(x_sample, params, x_sample.shape[0], x_sample.shape[1])
    return (y_prompt, y_sample)
```

```python
import functools

import numpy as np
import jax
import jax.numpy as jnp
from jax import lax
from jax.experimental import pallas as pl
from jax.experimental.pallas import tpu as pltpu

F32 = jnp.float32
BF16 = jnp.bfloat16

D_MODEL = 2048
DEPTH = 4
EPS = 1e-6
BLOCK = 128
D_A = 1024
G_A = 4
DG_A = D_A // G_A
HQ_B = 8
HKV_B = 2
HD_B = 128
D_B = HQ_B * HD_B
DKV_B = HKV_B * HD_B
GQ_B = HQ_B // HKV_B
ROPE_THETA = 10000.0
H_C = 4
DK_C = 512
DV_C = 1024
HDK_C = DK_C // H_C
HDV_C = DV_C // H_C
GATE_RANK = 16
GATE_TEMP = 16.0
CHUNK_C = 64
IN_SPLITS = (D_A, D_A, D_A, D_B, DKV_B, DKV_B, D_B, DK_C, DK_C, DV_C, DV_C,
             GATE_RANK, GATE_RANK, D_MODEL, D_MODEL, D_MODEL)
N_IN = sum(IN_SPLITS)
SEG_ORDER = (0, 1, 2, 3, 6, 10, 7, 8, 9, 13, 14, 15, 4, 5, 11, 12)

COL_A = 0
COL_QB = 3 * D_A
COL_ZB = COL_QB + D_B
COL_ZC = COL_ZB + D_B
COL_C = COL_ZC + DV_C
COL_GATE = COL_C + 2 * DK_C + DV_C
COL_KVB = COL_GATE + 3 * D_MODEL
COL_LR = COL_KVB + 2 * DKV_B
LANES = 128
N_PROJ = 15360
C_WIDTH = 2 * DK_C + DV_C

PREP_TC = 256
PROJ_TM = 1024
PROJ_TN = 1536
GLA_TB = 512
GLA_CUM = 256
GLA_LAG = 1
MM_TM = 2 * BLOCK
MERGE_TN = 512
NEG_BIG = -1e30
LOG2E = 1.4426950408889634
VMEM_LIMIT = 56 * 1024 * 1024


def _silu(z):
    return z * jax.nn.sigmoid(z)


def _prep_segments():
    offs = np.concatenate([[0], np.cumsum(IN_SPLITS)])
    segs, dst = [], 0
    for i in SEG_ORDER:
        segs.append((dst, int(offs[i]), IN_SPLITS[i]))
        dst += IN_SPLITS[i]
    return segs, dst


def _prep_w_in_kernel(w_ref, o_ref):
    segs, pad0 = _prep_segments()
    for dst, src, width in segs:
        o_ref[dst:dst + width, :] = w_ref[src:src + width, :].astype(o_ref.dtype)
    o_ref[pad0:, :] = jnp.zeros((N_PROJ - pad0, o_ref.shape[1]), o_ref.dtype)


def _prep_w_in(w_in):
    depth = w_in.shape[0]
    return pl.pallas_call(
        _prep_w_in_kernel,
        grid=(depth, D_MODEL // PREP_TC),
        in_specs=[pl.BlockSpec((None, N_IN, PREP_TC), lambda l, i: (l, 0, i))],
        out_specs=pl.BlockSpec((None, N_PROJ, PREP_TC), lambda l, i: (l, 0, i)),
        out_shape=jax.ShapeDtypeStruct((depth, N_PROJ, D_MODEL), BF16),
        compiler_params=pltpu.CompilerParams(vmem_limit_bytes=VMEM_LIMIT),
        name="prep_w_in",
    )(jnp.swapaxes(w_in, 1, 2))


def _rmsnorm_kernel(x_ref, g_ref, o_ref):
    x = x_ref[...]
    y = x * lax.rsqrt(jnp.mean(x * x, axis=-1, keepdims=True) + EPS) * g_ref[...]
    o_ref[...] = y.astype(o_ref.dtype)


def _rmsnorm(x, g, tm=512):
    t = x.shape[0]
    return pl.pallas_call(
        _rmsnorm_kernel,
        grid=(t // tm,),
        in_specs=[pl.BlockSpec((tm, D_MODEL), lambda i: (i, 0)),
                  pl.BlockSpec((1, D_MODEL), lambda i: (0, 0))],
        out_specs=pl.BlockSpec((tm, D_MODEL), lambda i: (i, 0)),
        out_shape=jax.ShapeDtypeStruct((t, D_MODEL), BF16),
        name="rmsnorm0",
    )(x, g)


def _proj_kernel(h_ref, wt_ref, o_ref):
    o_ref[...] = lax.dot_general(h_ref[...], wt_ref[...], (((1,), (1,)), ((), ())),
                                 preferred_element_type=F32).astype(o_ref.dtype)


def _proj(h, w_all, layer):
    t = h.shape[0]
    return pl.pallas_call(
        _proj_kernel,
        grid=(t // PROJ_TM, N_PROJ // PROJ_TN),
        in_specs=[pl.BlockSpec((PROJ_TM, D_MODEL), lambda i, j: (i, 0)),
                  pl.BlockSpec((None, PROJ_TN, D_MODEL), lambda i, j: (layer, j, 0))],
        out_specs=pl.BlockSpec((PROJ_TM, PROJ_TN), lambda i, j: (i, j)),
        out_shape=jax.ShapeDtypeStruct((t, N_PROJ), BF16),
        compiler_params=pltpu.CompilerParams(vmem_limit_bytes=VMEM_LIMIT),
        name="proj",
    )(h, w_all)


GLA_GROUP = 2 * CHUNK_C


def _gla_items(c_ref, lr_ref, w_ref, b_ref, o_ref, s_ref, *, reverse):
    nt = (((1,), (1,)), ((), ()))
    tn = (((0,), (0,)), ((), ()))
    last = 0 if reverse else CHUNK_C - 1
    n_groups = GLA_TB // GLA_GROUP
    order = list(range(n_groups - 1, -1, -1) if reverse else range(n_groups))
    env = {}

    def gates():
        x = jnp.dot(lr_ref[...], w_ref[...], preferred_element_type=F32) + b_ref[...]
        g = (jnp.minimum(x, 0.0) - jnp.log(1.0 + jnp.exp(-jnp.abs(x)))) * (1.0 / GATE_TEMP)
        r = lax.broadcasted_iota(jnp.int32, (GLA_CUM, GLA_CUM), 0)
        c = lax.broadcasted_iota(jnp.int32, (GLA_CUM, GLA_CUM), 1)
        shift = CHUNK_C.bit_length() - 1
        same_chunk = (r >> shift) == (c >> shift)
        tri = (c >= r) if reverse else (c <= r)
        cum = jnp.where(same_chunk, jnp.where(tri, 1.0, 0.0), 0.0).astype(BF16)
        g_hi = g.astype(BF16)
        g_lo = (g - g_hi.astype(F32)).astype(BF16)
        env["b"] = jnp.concatenate(
            [jnp.dot(cum, g_hi[i:i + GLA_CUM], preferred_element_type=F32)
             + jnp.dot(cum, g_lo[i:i + GLA_CUM], preferred_element_type=F32)
             for i in range(0, GLA_TB, GLA_CUM)], axis=0)
        env["s"] = [s_ref[h] for h in range(H_C)]

    def prep(gi):
        rows0 = slice(gi * GLA_GROUP, gi * GLA_GROUP + CHUNK_C)
        rows1 = slice(gi * GLA_GROUP + CHUNK_C, (gi + 1) * GLA_GROUP)
        b0, b1 = env["b"][rows0], env["b"][rows1]
        bl0, bl1 = b0[last:last + 1], b1[last:last + 1]
        dec0, dec1 = jnp.exp(bl0), jnp.exp(bl1)
        q0 = c_ref[rows0, 0:DK_C].astype(F32) * (HDK_C ** -0.5)
        q1 = c_ref[rows1, 0:DK_C].astype(F32) * (HDK_C ** -0.5)
        k0 = c_ref[rows0, DK_C:2 * DK_C].astype(F32)
        k1 = c_ref[rows1, DK_C:2 * DK_C].astype(F32)
        qe0, qe1 = q0 * jnp.exp(b0), q1 * jnp.exp(b1)
        ke0, ke1 = (k0 * jnp.exp(-b0)).astype(BF16), (k1 * jnp.exp(-b1)).astype(BF16)
        kd0, kd1 = k0 * jnp.exp(bl0 - b0), k1 * jnp.exp(bl1 - b1)
        if reverse:
            q_inter = jnp.concatenate([qe0 * dec1, qe1], axis=0).astype(BF16)
            k_state = jnp.concatenate([kd0, kd1 * dec0], axis=0).astype(BF16)
            keys0 = jnp.concatenate([ke0, kd1.astype(BF16)], axis=0)
            keys1 = jnp.concatenate([ke0, ke1], axis=0)
        else:
            q_inter = jnp.concatenate([qe0, qe1 * dec0], axis=0).astype(BF16)
            k_state = jnp.concatenate([kd0 * dec1, kd1], axis=0).astype(BF16)
            keys0 = jnp.concatenate([ke0, ke1], axis=0)
            keys1 = jnp.concatenate([kd0.astype(BF16), ke1], axis=0)
        env["prep", gi] = (qe0.astype(BF16), qe1.astype(BF16), keys0, keys1, q_inter, k_state,
                           dec0 * dec1)

    def intra(gi):
        qe0, qe1, keys0, keys1, q_inter, k_state, dec = env["prep", gi]
        rows = slice(gi * GLA_GROUP, (gi + 1) * GLA_GROUP)
        r2 = lax.broadcasted_iota(jnp.int32, (GLA_GROUP, GLA_GROUP), 0)
        c2 = lax.broadcasted_iota(jnp.int32, (GLA_GROUP, GLA_GROUP), 1)
        keep = (c2 > r2) if reverse else (c2 <= r2)
        per_head = []
        for h in range(H_C):
            kcols = slice(h * HDK_C, (h + 1) * HDK_C)
            v = c_ref[rows, 2 * DK_C + h * HDV_C:2 * DK_C + (h + 1) * HDV_C]
            a = jnp.concatenate(
                [lax.dot_general(qe0[:, kcols], keys0[:, kcols], nt, preferred_element_type=F32),
                 lax.dot_general(qe1[:, kcols], keys1[:, kcols], nt, preferred_element_type=F32)],
                axis=0)
            a = jnp.where(keep, a, 0.0).astype(BF16)
            ds_t = lax.dot_general(v, k_state[:, kcols], tn, preferred_element_type=F32)
            per_head.append((a, ds_t, q_inter[:, kcols], dec[:, kcols]))
        env["intra", gi] = per_head

    def scan(gi, final):
        rows = slice(gi * GLA_GROUP, (gi + 1) * GLA_GROUP)
        states = env["s"]
        for h, (a, ds_t, q_inter, dec) in enumerate(env["intra", gi]):
            v = c_ref[rows, 2 * DK_C + h * HDV_C:2 * DK_C + (h + 1) * HDV_C]
            o = (jnp.dot(a, v, preferred_element_type=F32)
                 + lax.dot_general(q_inter, states[h].astype(BF16), nt,
                                   preferred_element_type=F32))
            o_ref[rows, h * HDV_C:(h + 1) * HDV_C] = o.astype(o_ref.dtype)
            states[h] = dec * states[h] + ds_t
            if final:
                s_ref[h] = states[h]

    items = [gates]
    for gi in order:
        items += [functools.partial(prep, gi), functools.partial(intra, gi)]
    for n, gi in enumerate(order):
        items.append(functools.partial(scan, gi, n == len(order) - 1))
    return items


def _gla_kernel(cf_ref, lrf_ref, cb_ref, lrb_ref, wf_ref, bf_ref, wb_ref, bb_ref,
                of_ref, ob_ref, sf_ref, sb_ref):
    @pl.when(pl.program_id(1) == 0)
    def _():
        sf_ref[...] = jnp.zeros_like(sf_ref)
        sb_ref[...] = jnp.zeros_like(sb_ref)

    fwd = _gla_items(cf_ref, lrf_ref, wf_ref, bf_ref, of_ref, sf_ref, reverse=False)
    bwd = _gla_items(cb_ref, lrb_ref, wb_ref, bb_ref, ob_ref, sb_ref, reverse=True)
    lag = GLA_LAG
    for n in range(len(fwd) + lag):
        if n < len(fwd):
            fwd[n]()
        if 0 <= n - lag < len(bwd):
            bwd[n - lag]()


def _gla(proj, wf_pad, bf, wb_pad, bb, batch, seq):
    t = batch * seq
    nblk = seq // GLA_TB

    def fwd(b, j):
        return b * nblk + j

    def bwd(b, j):
        return b * nblk + (nblk - 1 - j)

    cblk = COL_C // C_WIDTH
    lrblk = COL_LR // LANES
    in_specs = [
        pl.BlockSpec((GLA_TB, C_WIDTH), lambda b, j: (fwd(b, j), cblk)),
        pl.BlockSpec((GLA_TB, LANES), lambda b, j: (fwd(b, j), lrblk)),
        pl.BlockSpec((GLA_TB, C_WIDTH), lambda b, j: (bwd(b, j), cblk)),
        pl.BlockSpec((GLA_TB, LANES), lambda b, j: (bwd(b, j), lrblk)),
        pl.BlockSpec((LANES, DK_C), lambda b, j: (0, 0)),
        pl.BlockSpec((1, DK_C), lambda b, j: (0, 0)),
        pl.BlockSpec((LANES, DK_C), lambda b, j: (0, 0)),
        pl.BlockSpec((1, DK_C), lambda b, j: (0, 0)),
    ]
    out_specs = [pl.BlockSpec((GLA_TB, DV_C), lambda b, j: (fwd(b, j), 0)),
                 pl.BlockSpec((GLA_TB, DV_C), lambda b, j: (bwd(b, j), 0))]
    return pl.pallas_call(
        _gla_kernel,
        grid=(batch, nblk),
        in_specs=in_specs,
        out_specs=out_specs,
        out_shape=[jax.ShapeDtypeStruct((t, DV_C), BF16), jax.ShapeDtypeStruct((t, DV_C), BF16)],
        scratch_shapes=[pltpu.VMEM((H_C, HDV_C, HDK_C), F32), pltpu.VMEM((H_C, HDV_C, HDK_C), F32)],
        compiler_params=pltpu.CompilerParams(dimension_semantics=("arbitrary", "arbitrary")),
        name="gla",
    )(proj, proj, proj, proj, wf_pad, bf, wb_pad, bb)


def _mix_a_items(a3_ref, lng_ref, lnb_ref, ws_ref, bias_ref, ya_ref):
    def item(blk):
        rows = slice(blk * BLOCK, (blk + 1) * BLOCK)
        va = a3_ref[rows, D_A:2 * D_A].astype(F32)
        mu = jnp.mean(va, axis=-1, keepdims=True)
        dv = va - mu
        var = jnp.mean(dv * dv, axis=-1, keepdims=True)
        vn = (dv * lax.rsqrt(var + EPS) * lng_ref[...] + lnb_ref[...]).astype(BF16)
        for g in range(G_A):
            cols = slice(g * DG_A, (g + 1) * DG_A)
            f = jnp.dot(ws_ref[g], vn[:, cols], preferred_element_type=F32) + bias_ref[:, cols]
            ua = a3_ref[rows, g * DG_A:(g + 1) * DG_A].astype(F32)
            za = a3_ref[rows, 2 * D_A + g * DG_A:2 * D_A + (g + 1) * DG_A].astype(F32)
            ya_ref[rows, cols] = (ua * f * _silu(za)).astype(ya_ref.dtype)

    return [functools.partial(item, blk) for blk in range(MM_TM // BLOCK)]


def _mix_b_items(sink_ref, q_ref, zb_ref, kvp_ref, kvc_ref, kvn_ref, csp_ref, csc_ref, csn_ref,
                 yb_ref, first, final):
    def rope(x, cs):
        return x * cs[:, :HD_B] + pltpu.roll(x, HD_B // 2, 1) * cs[:, HD_B:]

    q_scale = (HD_B ** -0.5) * LOG2E
    roped = {}

    def item(kh, blk):
        kcol = slice(kh * HD_B, (kh + 1) * HD_B)
        vcol = slice(DKV_B + kh * HD_B, DKV_B + (kh + 1) * HD_B)
        if kh not in roped:
            cs_k = jnp.concatenate([csp_ref[...], csc_ref[...], csn_ref[...]], axis=0)
            k_raw = jnp.concatenate([kvp_ref[:, kcol], kvc_ref[:, kcol], kvn_ref[:, kcol]], axis=0)
            k4 = rope(k_raw.astype(F32), cs_k).astype(BF16)
            v4 = jnp.concatenate([kvp_ref[:, vcol], kvc_ref[:, vcol], kvn_ref[:, vcol]], axis=0)
            roped[kh] = (k4, v4)
        k4, v4 = roped[kh]
        rows = slice(blk * BLOCK, (blk + 1) * BLOCK)
        il = lax.broadcasted_iota(jnp.int32, (BLOCK, 3 * BLOCK), 0)
        jc = lax.broadcasted_iota(jnp.int32, (BLOCK, 3 * BLOCK), 1)
        has_prev = 1 if blk > 0 else 1 - first
        has_next = 1 if blk < MM_TM // BLOCK - 1 else 1 - final
        lo = il * has_prev + BLOCK * (1 - has_prev)
        hi = (2 * BLOCK - 1) + (il + 1) * has_next
        mask_bias = jnp.where(jc >= lo, 0.0, NEG_BIG) + jnp.where(jc <= hi, 0.0, NEG_BIG)
        k3 = k4[blk * BLOCK:(blk + 3) * BLOCK]
        v3 = v4[blk * BLOCK:(blk + 3) * BLOCK]
        cs_q = csc_ref[rows, :]
        qs = jnp.concatenate(
            [rope(q_ref[rows, (kh * GQ_B + g) * HD_B:(kh * GQ_B + g + 1) * HD_B].astype(F32),
                  cs_q) * q_scale for g in range(GQ_B)], axis=0).astype(BF16)
        s = lax.dot_general(qs, k3, (((1,), (1,)), ((), ())), preferred_element_type=F32)
        for g in range(GQ_B):
            hq = kh * GQ_B + g
            sink = sink_ref[hq] * LOG2E
            sg = s[g * BLOCK:(g + 1) * BLOCK] + mask_bias
            m = jnp.maximum(jnp.max(sg, axis=-1, keepdims=True), sink)
            p = jnp.exp2(sg - m)
            den = jnp.sum(p, axis=-1, keepdims=True) + jnp.exp2(sink - m)
            o = jnp.dot(p.astype(BF16), v3, preferred_element_type=F32) / den
            hc = slice(hq * HD_B, (hq + 1) * HD_B)
            yb_ref[rows, hc] = (o * _silu(zb_ref[rows, hc].astype(F32))).astype(yb_ref.dtype)

    return [functools.partial(item, kh, blk)
            for kh in range(HKV_B) for blk in range(MM_TM // BLOCK)]


def _mix_c_items(of_ref, ob_ref, zc_ref, cg_ref, yc_ref):
    def item(h):
        hc = slice(h * HDV_C, (h + 1) * HDV_C)
        oh = of_ref[:, hc].astype(F32) + ob_ref[:, hc].astype(F32)
        ms = jnp.mean(oh * oh, axis=-1, keepdims=True)
        yc_ref[:, hc] = (oh * lax.rsqrt(ms + EPS) * cg_ref[...]
                         * _silu(zc_ref[:, hc].astype(F32))).astype(yc_ref.dtype)

    return [functools.partial(item, h) for h in range(H_C)]


def _merge_items(ya_ref, yb_ref, yc_ref, ga_ref, gb_ref, gc_ref, x_ref,
                 wpa_ref, wpb_ref, wpc_ref, wout_ref, ng_ref, x_out_ref, h_ref,
                 mg_s, xn_s, ss_s):
    n_chunks = D_MODEL // MERGE_TN

    def branch(c):
        cols = slice(c * MERGE_TN, (c + 1) * MERGE_TN)

        def gated(y_ref, w_ref, g_ref):
            return (jax.nn.sigmoid(g_ref[:, cols].astype(F32))
                    * jnp.dot(y_ref[...], w_ref[:, cols], preferred_element_type=F32))

        mg_s[:, cols] = (gated(ya_ref, wpa_ref, ga_ref) + gated(yb_ref, wpb_ref, gb_ref)
                         + gated(yc_ref, wpc_ref, gc_ref)).astype(mg_s.dtype)

    def out(c):
        cols = slice(c * MERGE_TN, (c + 1) * MERGE_TN)
        xn = x_ref[:, cols] + jnp.dot(mg_s[...], wout_ref[:, cols], preferred_element_type=F32)
        xn_s[:, cols] = xn
        if x_out_ref is not None:
            x_out_ref[:, cols] = xn
        part = jnp.sum(xn * xn, axis=-1, keepdims=True)
        ss_s[...] = part if c == 0 else ss_s[...] + part

    def norm():
        hn = xn_s[...] * lax.rsqrt(ss_s[...] * (1.0 / D_MODEL) + EPS) * ng_ref[...]
        h_ref[...] = hn.astype(h_ref.dtype)

    return ([functools.partial(branch, c) for c in range(n_chunks)]
            + [functools.partial(out, c) for c in range(n_chunks)] + [norm])


def _mixmerge_kernel(sink_ref, a3_ref, q_ref, zb_ref, kvp_ref, kvc_ref, kvn_ref,
                     csp_ref, csc_ref, csn_ref, lng_ref, lnb_ref, ws_ref, bias_ref,
                     of_ref, ob_ref, zc_ref, ga_ref, gb_ref, gc_ref, x_ref,
                     wpa_ref, wpb_ref, wpc_ref, wout_ref, cg_ref, ng_ref,
                     *rest, n_tiles, tiles_per_seq, emit_x):
    if emit_x:
        x_out_ref, h_ref, ya_s, yb_s, yc_s, mg_s, xn_s, ss_s = rest
    else:
        x_out_ref = None
        h_ref, ya_s, yb_s, yc_s, mg_s, xn_s, ss_s = rest
    r = pl.program_id(0)
    slot = lax.rem(r, 2)

    @pl.when(r == 0)
    def _():
        ya_s[1] = jnp.zeros(ya_s.shape[1:], ya_s.dtype)
        yb_s[1] = jnp.zeros(yb_s.shape[1:], yb_s.dtype)
        yc_s[1] = jnp.zeros(yc_s.shape[1:], yc_s.dtype)

    pos = lax.rem(jnp.minimum(r, n_tiles - 1), tiles_per_seq)
    first = (pos == 0).astype(jnp.int32)
    final = (pos == tiles_per_seq - 1).astype(jnp.int32)
    mix_a = _mix_a_items(a3_ref, lng_ref, lnb_ref, ws_ref, bias_ref, ya_s.at[slot])
    mix_b = _mix_b_items(sink_ref, q_ref, zb_ref, kvp_ref, kvc_ref, kvn_ref, csp_ref, csc_ref,
                         csn_ref, yb_s.at[slot], first, final)
    mix_c = _mix_c_items(of_ref, ob_ref, zc_ref, cg_ref, yc_s.at[slot])
    merge = _merge_items(ya_s.at[1 - slot], yb_s.at[1 - slot], yc_s.at[1 - slot], ga_ref, gb_ref,
                         gc_ref, x_ref, wpa_ref, wpb_ref, wpc_ref, wout_ref, ng_ref, x_out_ref,
                         h_ref, mg_s, xn_s, ss_s)
    mix = [[mix_c[0], mix_c[1]], [mix_a[0]], [mix_b[0]], [mix_b[1]], [mix_c[2], mix_c[3]],
           [mix_b[2]], [mix_b[3]], [mix_a[1]], []]
    for merge_item, mix_group in zip(merge, mix):
        merge_item()
        for mix_item in mix_group:
            mix_item()


def _mixmerge(proj, o_f, o_b, x, cs, params, layer, next_g, batch, seq, last):
    t = batch * seq
    n_tiles = t // MM_TM
    tps = seq // MM_TM
    nb2 = MM_TM // BLOCK

    def mix_tile(r):
        return jnp.minimum(r, n_tiles - 1)

    def merge_tile(r):
        return jnp.maximum(r - 1, 0)

    def kv_prev(r):
        m = mix_tile(r)
        return jnp.where(lax.rem(m, tps) == 0, nb2 * m, nb2 * m - 1)

    def kv_next(r):
        m = mix_tile(r)
        return jnp.where(lax.rem(m, tps) == tps - 1, nb2 * m + nb2 - 1, nb2 * m + nb2)

    def cs_prev(r):
        i = lax.rem(mix_tile(r), tps)
        return jnp.maximum(nb2 * i - 1, 0)

    def cs_next(r):
        i = lax.rem(mix_tile(r), tps)
        return jnp.minimum(nb2 * i + nb2, nb2 * tps - 1)

    mix = lambda width, blk: pl.BlockSpec((MM_TM, width), lambda r: (mix_tile(r), blk))
    mrg = lambda width, blk: pl.BlockSpec((MM_TM, width), lambda r: (merge_tile(r), blk))
    per_layer = lambda shape: pl.BlockSpec((None,) + shape, lambda r: (layer,) + (0,) * len(shape))
    weight = lambda shape: pl.BlockSpec((None,) + shape, lambda r: (layer,) + (0,) * len(shape),
                                        pipeline_mode=pl.Buffered(1))
    kvb = COL_KVB // (2 * DKV_B)
    gate0 = COL_GATE // D_MODEL
    in_specs = [
        pl.BlockSpec(memory_space=pltpu.SMEM),
        mix(3 * D_A, COL_A // (3 * D_A)), mix(D_B, COL_QB // D_B), mix(D_B, COL_ZB // D_B),
        pl.BlockSpec((BLOCK, 2 * DKV_B), lambda r: (kv_prev(r), kvb)),
        mix(2 * DKV_B, kvb),
        pl.BlockSpec((BLOCK, 2 * DKV_B), lambda r: (kv_next(r), kvb)),
        pl.BlockSpec((BLOCK, 2 * HD_B), lambda r: (cs_prev(r), 0)),
        pl.BlockSpec((MM_TM, 2 * HD_B), lambda r: (lax.rem(mix_tile(r), tps), 0)),
        pl.BlockSpec((BLOCK, 2 * HD_B), lambda r: (cs_next(r), 0)),
        per_layer((1, D_A)), per_layer((1, D_A)), per_layer((G_A, BLOCK, BLOCK)),
        per_layer((BLOCK, D_A)),
        mix(DV_C, 0), mix(DV_C, 0), mix(DV_C, COL_ZC // DV_C),
        mrg(D_MODEL, gate0), mrg(D_MODEL, gate0 + 1), mrg(D_MODEL, gate0 + 2),
        mrg(D_MODEL, 0),
        weight((D_A, D_MODEL)), weight((D_B, D_MODEL)), weight((DV_C, D_MODEL)),
        weight((D_MODEL, D_MODEL)), per_layer((1, HDV_C)),
        pl.BlockSpec((1, D_MODEL), lambda r: (0, 0)),
    ]
    h_dtype = F32 if last else BF16
    out_specs = [mrg(D_MODEL, 0)]
    out_shape = [jax.ShapeDtypeStruct((t, D_MODEL), h_dtype)]
    if not last:
        out_specs = [mrg(D_MODEL, 0)] + out_specs
        out_shape = [jax.ShapeDtypeStruct((t, D_MODEL), F32)] + out_shape
    outs = pl.pallas_call(
        functools.partial(_mixmerge_kernel, n_tiles=n_tiles, tiles_per_seq=tps, emit_x=not last),
        grid=(n_tiles + 1,),
        in_specs=in_specs,
        out_specs=out_specs,
        out_shape=out_shape,
        scratch_shapes=[pltpu.VMEM((2, MM_TM, D_A), BF16), pltpu.VMEM((2, MM_TM, D_B), BF16),
                        pltpu.VMEM((2, MM_TM, DV_C), BF16), pltpu.VMEM((MM_TM, D_MODEL), BF16),
                        pltpu.VMEM((MM_TM, D_MODEL), F32), pltpu.VMEM((MM_TM, 1), F32)],
        compiler_params=pltpu.CompilerParams(dimension_semantics=("arbitrary",),
                                             vmem_limit_bytes=VMEM_LIMIT),
        name="mixmerge",
    )(params["b_sink"][layer], proj, proj, proj, proj, proj, proj, cs, cs, cs,
      params["a_ln_g"], params["a_ln_b"], params["a_ws"], params["a_bias"],
      o_f, o_b, proj, proj, proj, proj, x,
      params["w_pa"], params["w_pb"], params["w_pc"], params["w_out"], params["c_norm_g"], next_g)
    return (None, outs[0]) if last else (outs[0], outs[1])


def _rope_table(seq):
    half = HD_B // 2
    inv = ROPE_THETA ** (-jnp.arange(half, dtype=F32) * 2.0 / HD_B)
    ang = jnp.arange(seq, dtype=F32)[:, None] * inv[None, :]
    cos, sin = jnp.cos(ang), jnp.sin(ang)
    return jnp.concatenate([cos, cos, -sin, sin], axis=-1)


def _pad_gate_w(w, row0):
    z = jnp.zeros((w.shape[0], LANES, DK_C), w.dtype)
    return z.at[:, row0:row0 + GATE_RANK].set(w).astype(BF16)


def _prepare(norm_g, w_in, a_ln_g, a_ln_b, a_ws, a_bs, b_sink, c_wf, c_bf, c_wb, c_bb, c_norm_g,
             w_pa, w_pb, w_pc, w_out, final_g):
    return dict(
        norm_g=norm_g, final_g=final_g, w_in=_prep_w_in(w_in),
        a_ln_g=a_ln_g[:, None], a_ln_b=a_ln_b[:, None], a_ws=a_ws.astype(BF16),
        a_bias=jnp.repeat(jnp.swapaxes(a_bs, 1, 2), DG_A, axis=2),
        b_sink=b_sink,
        c_wf=_pad_gate_w(c_wf, 0), c_bf=c_bf, c_wb=_pad_gate_w(c_wb, GATE_RANK), c_bb=c_bb,
        c_norm_g=c_norm_g[:, None], w_pa=w_pa.astype(BF16), w_pb=w_pb.astype(BF16),
        w_pc=w_pc.astype(BF16), w_out=w_out.astype(BF16),
    )


def _trunk(x, params, batch, seq):
    t = batch * seq
    x = x.reshape(t, D_MODEL)
    cs = _rope_table(seq)
    h = _rmsnorm(x, params["norm_g"][0][None])
    for l in range(DEPTH):
        last = l == DEPTH - 1
        proj = _proj(h, params["w_in"], l)
        o_f, o_b = _gla(proj, params["c_wf"][l], params["c_bf"][l][None], params["c_wb"][l],
                        params["c_bb"][l][None], batch, seq)
        next_g = params["final_g"] if last else params["norm_g"][l + 1]
        x, h = _mixmerge(proj, o_f, o_b, x, cs, params, l, next_g[None], batch, seq, last)
    return h.reshape(batch, seq, D_MODEL)


def kernel(x_prompt, x_sample, norm_g, w_in, a_ln_g, a_ln_b, a_ws, a_bs, b_sink, c_wf, c_bf,
           c_wb, c_bb, c_norm_g, w_pa, w_pb, w_pc, w_out, final_g):
    params = _prepare(norm_g, w_in, a_ln_g, a_ln_b, a_ws, a_bs, b_sink, c_wf, c_bf, c_wb, c_bb,
                      c_norm_g, w_pa, w_pb, w_pc, w_out, final_g)
    y_prompt = _trunk(x_prompt, params, x_prompt.shape[0], x_prompt.shape[1])
    y_sample = _trunk(x_sample, params, x_sample.shape[0], x_sample.shape[1])
    return (y_prompt, y_sample)
```

```python
import functools

import numpy as np
import jax
import jax.numpy as jnp
from jax import lax
from jax.experimental import pallas as pl
from jax.experimental.pallas import tpu as pltpu

F32 = jnp.float32
BF16 = jnp.bfloat16

D_MODEL = 2048
DEPTH = 4
EPS = 1e-6
BLOCK = 128
D_A = 1024
G_A = 4
DG_A = D_A // G_A
HQ_B = 8
HKV_B = 2
HD_B = 128
D_B = HQ_B * HD_B
DKV_B = HKV_B * HD_B
GQ_B = HQ_B // HKV_B
ROPE_THETA = 10000.0
H_C = 4
DK_C = 512
DV_C = 1024
HDK_C = DK_C // H_C
HDV_C = DV_C // H_C
GATE_RANK = 16
GATE_TEMP = 16.0
CHUNK_C = 64
IN_SPLITS = (D_A, D_A, D_A, D_B, DKV_B, DKV_B, D_B, DK_C, DK_C, DV_C, DV_C,
             GATE_RANK, GATE_RANK, D_MODEL, D_MODEL, D_MODEL)
N_IN = sum(IN_SPLITS)
SEG_ORDER = (0, 1, 2, 3, 6, 10, 13, 14, 15, 7, 8, 9, 4, 5, 11, 12)

COL_MIX = 0
MIX_WIDTH = 3 * D_A + 2 * D_B + DV_C
COL_GATE = COL_MIX + MIX_WIDTH
COL_C = COL_GATE + 3 * D_MODEL
COL_KVB = COL_C + 2 * DK_C + DV_C
COL_LR = COL_KVB + 2 * DKV_B
LANES = 128
N_PROJ = 15360
C_WIDTH = 2 * DK_C + DV_C

PREP_TC = 256
PROJ_TM = 2048
PROJ_TN = 1536
GLA_TB = 512
GLA_CUM = 256
GLA_LAG = 1
MM_TM = 2 * BLOCK
MERGE_TN = 512
NEG_BIG = -1e30
LOG2E = 1.4426950408889634
VMEM_LIMIT = 56 * 1024 * 1024


def _silu(z):
    return z * jax.nn.sigmoid(z)


def _prep_segments():
    offs = np.concatenate([[0], np.cumsum(IN_SPLITS)])
    segs, dst = [], 0
    for i in SEG_ORDER:
        segs.append((dst, int(offs[i]), IN_SPLITS[i]))
        dst += IN_SPLITS[i]
    return segs, dst


def _prep_w_in_kernel(w_ref, o_ref):
    segs, pad0 = _prep_segments()
    for dst, src, width in segs:
        o_ref[dst:dst + width, :] = w_ref[src:src + width, :].astype(o_ref.dtype)
    o_ref[pad0:, :] = jnp.zeros((N_PROJ - pad0, o_ref.shape[1]), o_ref.dtype)


def _prep_w_in(w_in):
    depth = w_in.shape[0]
    return pl.pallas_call(
        _prep_w_in_kernel,
        grid=(depth, D_MODEL // PREP_TC),
        in_specs=[pl.BlockSpec((None, N_IN, PREP_TC), lambda l, i: (l, 0, i))],
        out_specs=pl.BlockSpec((None, N_PROJ, PREP_TC), lambda l, i: (l, 0, i)),
        out_shape=jax.ShapeDtypeStruct((depth, N_PROJ, D_MODEL), BF16),
        compiler_params=pltpu.CompilerParams(vmem_limit_bytes=VMEM_LIMIT),
        name="prep_w_in",
    )(jnp.swapaxes(w_in, 1, 2))


def _rmsnorm_kernel(x_ref, g_ref, o_ref):
    x = x_ref[...]
    y = x * lax.rsqrt(jnp.mean(x * x, axis=-1, keepdims=True) + EPS) * g_ref[...]
    o_ref[...] = y.astype(o_ref.dtype)


def _rmsnorm(x, g, tm=512):
    t = x.shape[0]
    return pl.pallas_call(
        _rmsnorm_kernel,
        grid=(t // tm,),
        in_specs=[pl.BlockSpec((tm, D_MODEL), lambda i: (i, 0)),
                  pl.BlockSpec((1, D_MODEL), lambda i: (0, 0))],
        out_specs=pl.BlockSpec((tm, D_MODEL), lambda i: (i, 0)),
        out_shape=jax.ShapeDtypeStruct((t, D_MODEL), BF16),
        name="rmsnorm0",
    )(x, g)


def _proj_kernel(h_ref, wt_ref, o_ref):
    o_ref[...] = lax.dot_general(h_ref[...], wt_ref[...], (((1,), (1,)), ((), ())),
                                 preferred_element_type=F32).astype(o_ref.dtype)


def _proj(h, w_all, layer):
    t = h.shape[0]
    return pl.pallas_call(
        _proj_kernel,
        grid=(t // PROJ_TM, N_PROJ // PROJ_TN),
        in_specs=[pl.BlockSpec((PROJ_TM, D_MODEL), lambda i, j: (i, 0)),
                  pl.BlockSpec((None, PROJ_TN, D_MODEL), lambda i, j: (layer, j, 0))],
        out_specs=pl.BlockSpec((PROJ_TM, PROJ_TN), lambda i, j: (i, j)),
        out_shape=jax.ShapeDtypeStruct((t, N_PROJ), BF16),
        compiler_params=pltpu.CompilerParams(vmem_limit_bytes=VMEM_LIMIT),
        name="proj",
    )(h, w_all)


GLA_GROUP = 2 * CHUNK_C


def _gla_items(c_ref, lr_ref, w_ref, b_ref, o_ref, s_ref, *, reverse):
    nt = (((1,), (1,)), ((), ()))
    tn = (((0,), (0,)), ((), ()))
    last = 0 if reverse else CHUNK_C - 1
    n_groups = GLA_TB // GLA_GROUP
    order = list(range(n_groups - 1, -1, -1) if reverse else range(n_groups))
    env = {}

    def gates():
        x = jnp.dot(lr_ref[...], w_ref[...], preferred_element_type=F32) + b_ref[...]
        g = (jnp.minimum(x, 0.0) - jnp.log(1.0 + jnp.exp(-jnp.abs(x)))) * (1.0 / GATE_TEMP)
        r = lax.broadcasted_iota(jnp.int32, (GLA_CUM, GLA_CUM), 0)
        c = lax.broadcasted_iota(jnp.int32, (GLA_CUM, GLA_CUM), 1)
        shift = CHUNK_C.bit_length() - 1
        same_chunk = (r >> shift) == (c >> shift)
        tri = (c >= r) if reverse else (c <= r)
        cum = jnp.where(same_chunk, jnp.where(tri, 1.0, 0.0), 0.0).astype(BF16)
        g_hi = g.astype(BF16)
        g_lo = (g - g_hi.astype(F32)).astype(BF16)
        env["b"] = jnp.concatenate(
            [jnp.dot(cum, g_hi[i:i + GLA_CUM], preferred_element_type=F32)
             + jnp.dot(cum, g_lo[i:i + GLA_CUM], preferred_element_type=F32)
             for i in range(0, GLA_TB, GLA_CUM)], axis=0)
        env["s"] = [s_ref[h] for h in range(H_C)]

    def prep(gi):
        rows0 = slice(gi * GLA_GROUP, gi * GLA_GROUP + CHUNK_C)
        rows1 = slice(gi * GLA_GROUP + CHUNK_C, (gi + 1) * GLA_GROUP)
        b0, b1 = env["b"][rows0], env["b"][rows1]
        bl0, bl1 = b0[last:last + 1], b1[last:last + 1]
        dec0, dec1 = jnp.exp(bl0), jnp.exp(bl1)
        q0 = c_ref[rows0, 0:DK_C].astype(F32) * (HDK_C ** -0.5)
        q1 = c_ref[rows1, 0:DK_C].astype(F32) * (HDK_C ** -0.5)
        k0 = c_ref[rows0, DK_C:2 * DK_C].astype(F32)
        k1 = c_ref[rows1, DK_C:2 * DK_C].astype(F32)
        qe0, qe1 = q0 * jnp.exp(b0), q1 * jnp.exp(b1)
        ke0, ke1 = (k0 * jnp.exp(-b0)).astype(BF16), (k1 * jnp.exp(-b1)).astype(BF16)
        kd0, kd1 = k0 * jnp.exp(bl0 - b0), k1 * jnp.exp(bl1 - b1)
        if reverse:
            q_inter = jnp.concatenate([qe0 * dec1, qe1], axis=0).astype(BF16)
            k_state = jnp.concatenate([kd0, kd1 * dec0], axis=0).astype(BF16)
            keys0 = jnp.concatenate([ke0, kd1.astype(BF16)], axis=0)
            keys1 = jnp.concatenate([ke0, ke1], axis=0)
        else:
            q_inter = jnp.concatenate([qe0, qe1 * dec0], axis=0).astype(BF16)
            k_state = jnp.concatenate([kd0 * dec1, kd1], axis=0).astype(BF16)
            keys0 = jnp.concatenate([ke0, ke1], axis=0)
            keys1 = jnp.concatenate([kd0.astype(BF16), ke1], axis=0)
        env["prep", gi] = (qe0.astype(BF16), qe1.astype(BF16), keys0, keys1, q_inter, k_state,
                           dec0 * dec1)

    def intra(gi):
        qe0, qe1, keys0, keys1, q_inter, k_state, dec = env["prep", gi]
        rows = slice(gi * GLA_GROUP, (gi + 1) * GLA_GROUP)
        r2 = lax.broadcasted_iota(jnp.int32, (GLA_GROUP, GLA_GROUP), 0)
        c2 = lax.broadcasted_iota(jnp.int32, (GLA_GROUP, GLA_GROUP), 1)
        keep = (c2 > r2) if reverse else (c2 <= r2)
        per_head = []
        for h in range(H_C):
            kcols = slice(h * HDK_C, (h + 1) * HDK_C)
            v = c_ref[rows, 2 * DK_C + h * HDV_C:2 * DK_C + (h + 1) * HDV_C]
            a = jnp.concatenate(
                [lax.dot_general(qe0[:, kcols], keys0[:, kcols], nt, preferred_element_type=F32),
                 lax.dot_general(qe1[:, kcols], keys1[:, kcols], nt, preferred_element_type=F32)],
                axis=0)
            a = jnp.where(keep, a, 0.0).astype(BF16)
            ds = lax.dot_general(k_state[:, kcols], v, tn, preferred_element_type=F32)
            dec_col = jnp.transpose(jnp.broadcast_to(dec[:, kcols], (8, HDK_C)))[:, 0:1]
            lhs = jnp.concatenate([a, q_inter[:, kcols]], axis=1)
            per_head.append((lhs, ds, dec_col))
        env["intra", gi] = per_head

    def scan(gi, final):
        rows = slice(gi * GLA_GROUP, (gi + 1) * GLA_GROUP)
        states = env["s"]
        for h, (lhs, ds, dec_col) in enumerate(env["intra", gi]):
            v = c_ref[rows, 2 * DK_C + h * HDV_C:2 * DK_C + (h + 1) * HDV_C]
            rhs = jnp.concatenate([v, states[h].astype(BF16)], axis=0)
            o = jnp.dot(lhs, rhs, preferred_element_type=F32)
            o_ref[rows, h * HDV_C:(h + 1) * HDV_C] = o.astype(o_ref.dtype)
            states[h] = dec_col * states[h] + ds
            if final:
                s_ref[h] = states[h]

    items = [gates]
    for gi in order:
        items += [functools.partial(prep, gi), functools.partial(intra, gi)]
    for n, gi in enumerate(order):
        items.append(functools.partial(scan, gi, n == len(order) - 1))
    return items


def _gla_kernel(cf_ref, lrf_ref, cb_ref, lrb_ref, wf_ref, bf_ref, wb_ref, bb_ref,
                of_ref, ob_ref, sf_ref, sb_ref):
    @pl.when(pl.program_id(1) == 0)
    def _():
        sf_ref[...] = jnp.zeros_like(sf_ref)
        sb_ref[...] = jnp.zeros_like(sb_ref)

    fwd = _gla_items(cf_ref, lrf_ref, wf_ref, bf_ref, of_ref, sf_ref, reverse=False)
    bwd = _gla_items(cb_ref, lrb_ref, wb_ref, bb_ref, ob_ref, sb_ref, reverse=True)
    lag = GLA_LAG
    for n in range(len(fwd) + lag):
        if n < len(fwd):
            fwd[n]()
        if 0 <= n - lag < len(bwd):
            bwd[n - lag]()


def _gla(proj, wf_pad, bf, wb_pad, bb, batch, seq):
    t = batch * seq
    nblk = seq // GLA_TB

    def fwd(b, j):
        return b * nblk + j

    def bwd(b, j):
        return b * nblk + (nblk - 1 - j)

    cblk = COL_C // C_WIDTH
    lrblk = COL_LR // LANES
    in_specs = [
        pl.BlockSpec((GLA_TB, C_WIDTH), lambda b, j: (fwd(b, j), cblk)),
        pl.BlockSpec((GLA_TB, LANES), lambda b, j: (fwd(b, j), lrblk)),
        pl.BlockSpec((GLA_TB, C_WIDTH), lambda b, j: (bwd(b, j), cblk)),
        pl.BlockSpec((GLA_TB, LANES), lambda b, j: (bwd(b, j), lrblk)),
        pl.BlockSpec((LANES, DK_C), lambda b, j: (0, 0)),
        pl.BlockSpec((1, DK_C), lambda b, j: (0, 0)),
        pl.BlockSpec((LANES, DK_C), lambda b, j: (0, 0)),
        pl.BlockSpec((1, DK_C), lambda b, j: (0, 0)),
    ]
    out_specs = [pl.BlockSpec((GLA_TB, DV_C), lambda b, j: (fwd(b, j), 0)),
                 pl.BlockSpec((GLA_TB, DV_C), lambda b, j: (bwd(b, j), 0))]
    return pl.pallas_call(
        _gla_kernel,
        grid=(batch, nblk),
        in_specs=in_specs,
        out_specs=out_specs,
        out_shape=[jax.ShapeDtypeStruct((t, DV_C), BF16), jax.ShapeDtypeStruct((t, DV_C), BF16)],
        scratch_shapes=[pltpu.VMEM((H_C, HDK_C, HDV_C), F32), pltpu.VMEM((H_C, HDK_C, HDV_C), F32)],
        compiler_params=pltpu.CompilerParams(dimension_semantics=("arbitrary", "arbitrary")),
        name="gla",
    )(proj, proj, proj, proj, wf_pad, bf, wb_pad, bb)


def _mix_a_items(a3_ref, lng_ref, lnb_ref, ws_ref, bias_ref, ya_ref):
    env = {}

    def norm(blk):
        rows = slice(blk * BLOCK, (blk + 1) * BLOCK)
        va = a3_ref[rows, D_A:2 * D_A].astype(F32)
        mu = jnp.mean(va, axis=-1, keepdims=True)
        dv = va - mu
        var = jnp.mean(dv * dv, axis=-1, keepdims=True)
        env[blk] = (dv * lax.rsqrt(var + EPS) * lng_ref[...] + lnb_ref[...]).astype(BF16)

    def group(blk, g):
        rows = slice(blk * BLOCK, (blk + 1) * BLOCK)
        cols = slice(g * DG_A, (g + 1) * DG_A)
        f = jnp.dot(ws_ref[g], env[blk][:, cols], preferred_element_type=F32) + bias_ref[:, cols]
        ua = a3_ref[rows, g * DG_A:(g + 1) * DG_A].astype(F32)
        za = a3_ref[rows, 2 * D_A + g * DG_A:2 * D_A + (g + 1) * DG_A].astype(F32)
        ya_ref[rows, cols] = (ua * f * _silu(za)).astype(ya_ref.dtype)

    items = []
    for blk in range(MM_TM // BLOCK):
        items.append(functools.partial(norm, blk))
        items += [functools.partial(group, blk, g) for g in range(G_A)]
    return items


def _mix_b_items(sink_ref, q_ref, zb_ref, kvp_ref, kvc_ref, kvn_ref, csp_ref, csc_ref, csn_ref,
                 yb_ref, first, final):
    def rope(x, cs):
        return x * cs[:, :HD_B] + pltpu.roll(x, HD_B // 2, 1) * cs[:, HD_B:]

    q_scale = (HD_B ** -0.5) * LOG2E
    roped = {}
    env = {}

    def logits(kh, blk):
        kcol = slice(kh * HD_B, (kh + 1) * HD_B)
        vcol = slice(DKV_B + kh * HD_B, DKV_B + (kh + 1) * HD_B)
        if kh not in roped:
            cs_k = jnp.concatenate([csp_ref[...], csc_ref[...], csn_ref[...]], axis=0)
            k_raw = jnp.concatenate([kvp_ref[:, kcol], kvc_ref[:, kcol], kvn_ref[:, kcol]], axis=0)
            k4 = rope(k_raw.astype(F32), cs_k).astype(BF16)
            v4 = jnp.concatenate([kvp_ref[:, vcol], kvc_ref[:, vcol], kvn_ref[:, vcol]], axis=0)
            roped[kh] = (k4, v4)
        k4, v4 = roped[kh]
        rows = slice(blk * BLOCK, (blk + 1) * BLOCK)
        il = lax.broadcasted_iota(jnp.int32, (BLOCK, 3 * BLOCK), 0)
        jc = lax.broadcasted_iota(jnp.int32, (BLOCK, 3 * BLOCK), 1)
        has_prev = 1 if blk > 0 else 1 - first
        has_next = 1 if blk < MM_TM // BLOCK - 1 else 1 - final
        lo = il * has_prev + BLOCK * (1 - has_prev)
        hi = (2 * BLOCK - 1) + (il + 1) * has_next
        mask_bias = jnp.where(jc >= lo, 0.0, NEG_BIG) + jnp.where(jc <= hi, 0.0, NEG_BIG)
        k3 = k4[blk * BLOCK:(blk + 3) * BLOCK]
        v3 = v4[blk * BLOCK:(blk + 3) * BLOCK]
        cs_q = csc_ref[rows, :]
        qs = jnp.concatenate(
            [rope(q_ref[rows, (kh * GQ_B + g) * HD_B:(kh * GQ_B + g + 1) * HD_B].astype(F32),
                  cs_q) * q_scale for g in range(GQ_B)], axis=0).astype(BF16)
        s = lax.dot_general(qs, k3, (((1,), (1,)), ((), ())), preferred_element_type=F32)
        env[kh, blk] = (s, mask_bias, v3)

    def head(kh, blk, g):
        s, mask_bias, v3 = env[kh, blk]
        rows = slice(blk * BLOCK, (blk + 1) * BLOCK)
        hq = kh * GQ_B + g
        sink = sink_ref[hq] * LOG2E
        sg = s[g * BLOCK:(g + 1) * BLOCK] + mask_bias
        m = jnp.maximum(jnp.max(sg, axis=-1, keepdims=True), sink)
        p = jnp.exp2(sg - m)
        den = jnp.sum(p, axis=-1, keepdims=True) + jnp.exp2(sink - m)
        o = jnp.dot(p.astype(BF16), v3, preferred_element_type=F32) / den
        hc = slice(hq * HD_B, (hq + 1) * HD_B)
        yb_ref[rows, hc] = (o * _silu(zb_ref[rows, hc].astype(F32))).astype(yb_ref.dtype)

    items = []
    for kh in range(HKV_B):
        for blk in range(MM_TM // BLOCK):
            items.append(functools.partial(logits, kh, blk))
            items += [functools.partial(head, kh, blk, g) for g in range(GQ_B)]
    return items


def _mix_c_items(of_ref, ob_ref, zc_ref, cg_ref, yc_ref):
    def item(h):
        hc = slice(h * HDV_C, (h + 1) * HDV_C)
        oh = of_ref[:, hc].astype(F32) + ob_ref[:, hc].astype(F32)
        ms = jnp.mean(oh * oh, axis=-1, keepdims=True)
        yc_ref[:, hc] = (oh * lax.rsqrt(ms + EPS) * cg_ref[...]
                         * _silu(zc_ref[:, hc].astype(F32))).astype(yc_ref.dtype)

    return [functools.partial(item, h) for h in range(H_C)]


def _merge_items(ya_ref, yb_ref, yc_ref, ga_ref, gb_ref, gc_ref, x_ref,
                 wpa_ref, wpb_ref, wpc_ref, wout_ref, ng_ref, x_out_ref, h_ref,
                 mg_s, xn_s, ss_s):
    n_chunks = D_MODEL // MERGE_TN

    branches = ((ya_ref, wpa_ref, ga_ref), (yb_ref, wpb_ref, gb_ref), (yc_ref, wpc_ref, gc_ref))
    env = {}

    def branch(c, k):
        cols = slice(c * MERGE_TN, (c + 1) * MERGE_TN)
        y_ref, w_ref, g_ref = branches[k]
        part = (jax.nn.sigmoid(g_ref[:, cols].astype(F32))
                * jnp.dot(y_ref[...], w_ref[:, cols], preferred_element_type=F32))
        acc = part if k == 0 else env[c] + part
        if k == len(branches) - 1:
            mg_s[:, cols] = acc.astype(mg_s.dtype)
        else:
            env[c] = acc

    def out(c):
        cols = slice(c * MERGE_TN, (c + 1) * MERGE_TN)
        xn = x_ref[:, cols] + jnp.dot(mg_s[...], wout_ref[:, cols], preferred_element_type=F32)
        xn_s[:, cols] = xn
        if x_out_ref is not None:
            x_out_ref[:, cols] = xn
        part = jnp.sum(xn * xn, axis=-1, keepdims=True)
        ss_s[...] = part if c == 0 else ss_s[...] + part

    def norm():
        hn = xn_s[...] * lax.rsqrt(ss_s[...] * (1.0 / D_MODEL) + EPS) * ng_ref[...]
        h_ref[...] = hn.astype(h_ref.dtype)

    return ([functools.partial(branch, c, k) for c in range(n_chunks) for k in range(len(branches))]
            + [functools.partial(out, c) for c in range(n_chunks)] + [norm])


def _mixmerge_kernel(sink_ref, m_ref, kvp_ref, kvc_ref, kvn_ref,
                     csp_ref, csc_ref, csn_ref, lng_ref, lnb_ref, ws_ref, bias_ref,
                     of_ref, ob_ref, g_ref, x_ref,
                     wpa_ref, wpb_ref, wpc_ref, wout_ref, cg_ref, ng_ref,
                     *rest, n_tiles, tiles_per_seq, emit_x):
    if emit_x:
        x_out_ref, h_ref, ya_s, yb_s, yc_s, mg_s, xn_s, ss_s = rest
    else:
        x_out_ref = None
        h_ref, ya_s, yb_s, yc_s, mg_s, xn_s, ss_s = rest
    a3_ref = m_ref.at[:, 0:3 * D_A]
    q_ref = m_ref.at[:, 3 * D_A:3 * D_A + D_B]
    zb_ref = m_ref.at[:, 3 * D_A + D_B:3 * D_A + 2 * D_B]
    zc_ref = m_ref.at[:, 3 * D_A + 2 * D_B:MIX_WIDTH]
    ga_ref, gb_ref, gc_ref = [g_ref.at[:, k * D_MODEL:(k + 1) * D_MODEL] for k in range(3)]
    r = pl.program_id(0)
    slot = lax.rem(r, 2)

    @pl.when(r == 0)
    def _():
        ya_s[1] = jnp.zeros(ya_s.shape[1:], ya_s.dtype)
        yb_s[1] = jnp.zeros(yb_s.shape[1:], yb_s.dtype)
        yc_s[1] = jnp.zeros(yc_s.shape[1:], yc_s.dtype)

    pos = lax.rem(jnp.minimum(r, n_tiles - 1), tiles_per_seq)
    first = (pos == 0).astype(jnp.int32)
    final = (pos == tiles_per_seq - 1).astype(jnp.int32)
    mix_a = _mix_a_items(a3_ref, lng_ref, lnb_ref, ws_ref, bias_ref, ya_s.at[slot])
    mix_b = _mix_b_items(sink_ref, q_ref, zb_ref, kvp_ref, kvc_ref, kvn_ref, csp_ref, csc_ref,
                         csn_ref, yb_s.at[slot], first, final)
    mix_c = _mix_c_items(of_ref, ob_ref, zc_ref, cg_ref, yc_s.at[slot])
    merge = _merge_items(ya_s.at[1 - slot], yb_s.at[1 - slot], yc_s.at[1 - slot], ga_ref, gb_ref,
                         gc_ref, x_ref, wpa_ref, wpb_ref, wpc_ref, wout_ref, ng_ref, x_out_ref,
                         h_ref, mg_s, xn_s, ss_s)
    mix = mix_c[:2] + mix_b[:10] + mix_a[:5] + mix_c[2:] + mix_b[10:] + mix_a[5:]
    slots = len(merge) - 1
    done = 0
    for n, merge_item in enumerate(merge):
        merge_item()
        upto = len(mix) if n >= slots - 1 else (len(mix) * (n + 1)) // slots
        for mix_item in mix[done:upto]:
            mix_item()
        done = upto


def _mixmerge(proj, o_f, o_b, x, cs, params, layer, next_g, batch, seq, last):
    t = batch * seq
    n_tiles = t // MM_TM
    tps = seq // MM_TM
    nb2 = MM_TM // BLOCK

    def mix_tile(r):
        return jnp.minimum(r, n_tiles - 1)

    def merge_tile(r):
        return jnp.maximum(r - 1, 0)

    def kv_prev(r):
        m = mix_tile(r)
        return jnp.where(lax.rem(m, tps) == 0, nb2 * m, nb2 * m - 1)

    def kv_next(r):
        m = mix_tile(r)
        return jnp.where(lax.rem(m, tps) == tps - 1, nb2 * m + nb2 - 1, nb2 * m + nb2)

    def cs_prev(r):
        i = lax.rem(mix_tile(r), tps)
        return jnp.maximum(nb2 * i - 1, 0)

    def cs_next(r):
        i = lax.rem(mix_tile(r), tps)
        return jnp.minimum(nb2 * i + nb2, nb2 * tps - 1)

    mix = lambda width, blk: pl.BlockSpec((MM_TM, width), lambda r: (mix_tile(r), blk))
    mrg = lambda width, blk: pl.BlockSpec((MM_TM, width), lambda r: (merge_tile(r), blk))
    per_layer = lambda shape: pl.BlockSpec((None,) + shape, lambda r: (layer,) + (0,) * len(shape))
    weight = lambda shape: pl.BlockSpec((None,) + shape, lambda r: (layer,) + (0,) * len(shape),
                                        pipeline_mode=pl.Buffered(1))
    kvb = COL_KVB // (2 * DKV_B)
    in_specs = [
        pl.BlockSpec(memory_space=pltpu.SMEM),
        mix(MIX_WIDTH, COL_MIX // MIX_WIDTH),
        pl.BlockSpec((BLOCK, 2 * DKV_B), lambda r: (kv_prev(r), kvb)),
        mix(2 * DKV_B, kvb),
        pl.BlockSpec((BLOCK, 2 * DKV_B), lambda r: (kv_next(r), kvb)),
        pl.BlockSpec((BLOCK, 2 * HD_B), lambda r: (cs_prev(r), 0)),
        pl.BlockSpec((MM_TM, 2 * HD_B), lambda r: (lax.rem(mix_tile(r), tps), 0)),
        pl.BlockSpec((BLOCK, 2 * HD_B), lambda r: (cs_next(r), 0)),
        per_layer((1, D_A)), per_layer((1, D_A)), per_layer((G_A, BLOCK, BLOCK)),
        per_layer((BLOCK, D_A)),
        mix(DV_C, 0), mix(DV_C, 0),
        mrg(3 * D_MODEL, COL_GATE // (3 * D_MODEL)),
        mrg(D_MODEL, 0),
        weight((D_A, D_MODEL)), weight((D_B, D_MODEL)), weight((DV_C, D_MODEL)),
        weight((D_MODEL, D_MODEL)), per_layer((1, HDV_C)),
        pl.BlockSpec((1, D_MODEL), lambda r: (0, 0)),
    ]
    h_dtype = F32 if last else BF16
    out_specs = [mrg(D_MODEL, 0)]
    out_shape = [jax.ShapeDtypeStruct((t, D_MODEL), h_dtype)]
    if not last:
        out_specs = [mrg(D_MODEL, 0)] + out_specs
        out_shape = [jax.ShapeDtypeStruct((t, D_MODEL), F32)] + out_shape
    outs = pl.pallas_call(
        functools.partial(_mixmerge_kernel, n_tiles=n_tiles, tiles_per_seq=tps, emit_x=not last),
        grid=(n_tiles + 1,),
        in_specs=in_specs,
        out_specs=out_specs,
        out_shape=out_shape,
        scratch_shapes=[pltpu.VMEM((2, MM_TM, D_A), BF16), pltpu.VMEM((2, MM_TM, D_B), BF16),
                        pltpu.VMEM((2, MM_TM, DV_C), BF16), pltpu.VMEM((MM_TM, D_MODEL), BF16),
                        pltpu.VMEM((MM_TM, D_MODEL), F32), pltpu.VMEM((MM_TM, 1), F32)],
        compiler_params=pltpu.CompilerParams(dimension_semantics=("arbitrary",),
                                             vmem_limit_bytes=VMEM_LIMIT),
        name="mixmerge",
    )(params["b_sink"][layer], proj, proj, proj, proj, cs, cs, cs,
      params["a_ln_g"], params["a_ln_b"], params["a_ws"], params["a_bias"],
      o_f, o_b, proj, x,
      params["w_pa"], params["w_pb"], params["w_pc"], params["w_out"], params["c_norm_g"], next_g)
    return (None, outs[0]) if last else (outs[0], outs[1])


def _rope_table(seq):
    half = HD_B // 2
    inv = ROPE_THETA ** (-jnp.arange(half, dtype=F32) * 2.0 / HD_B)
    ang = jnp.arange(seq, dtype=F32)[:, None] * inv[None, :]
    cos, sin = jnp.cos(ang), jnp.sin(ang)
    return jnp.concatenate([cos, cos, -sin, sin], axis=-1)


def _pad_gate_w(w, row0):
    z = jnp.zeros((w.shape[0], LANES, DK_C), w.dtype)
    return z.at[:, row0:row0 + GATE_RANK].set(w).astype(BF16)


def _prepare(norm_g, w_in, a_ln_g, a_ln_b, a_ws, a_bs, b_sink, c_wf, c_bf, c_wb, c_bb, c_norm_g,
             w_pa, w_pb, w_pc, w_out, final_g):
    return dict(
        norm_g=norm_g, final_g=final_g, w_in=_prep_w_in(w_in),
        a_ln_g=a_ln_g[:, None], a_ln_b=a_ln_b[:, None], a_ws=a_ws.astype(BF16),
        a_bias=jnp.repeat(jnp.swapaxes(a_bs, 1, 2), DG_A, axis=2),
        b_sink=b_sink,
        c_wf=_pad_gate_w(c_wf, 0), c_bf=c_bf, c_wb=_pad_gate_w(c_wb, GATE_RANK), c_bb=c_bb,
        c_norm_g=c_norm_g[:, None], w_pa=w_pa.astype(BF16), w_pb=w_pb.astype(BF16),
        w_pc=w_pc.astype(BF16), w_out=w_out.astype(BF16),
    )


def _trunk(x, params, batch, seq):
    t = batch * seq
    x = x.reshape(t, D_MODEL)
    cs = _rope_table(seq)
    h = _rmsnorm(x, params["norm_g"][0][None])
    for l in range(DEPTH):
        last = l == DEPTH - 1
        proj = _proj(h, params["w_in"], l)
        o_f, o_b = _gla(proj, params["c_wf"][l], params["c_bf"][l][None], params["c_wb"][l],
                        params["c_bb"][l][None], batch, seq)
        next_g = params["final_g"] if last else params["norm_g"][l + 1]
        x, h = _mixmerge(proj, o_f, o_b, x, cs, params, l, next_g[None], batch, seq, last)
    return h.reshape(batch, seq, D_MODEL)


def kernel(x_prompt, x_sample, norm_g, w_in, a_ln_g, a_ln_b, a_ws, a_bs, b_sink, c_wf, c_bf,
           c_wb, c_bb, c_norm_g, w_pa, w_pb, w_pc, w_out, final_g):
    params = _prepare(norm_g, w_in, a_ln_g, a_ln_b, a_ws, a_bs, b_sink, c_wf, c_bf, c_wb, c_bb,
                      c_norm_g, w_pa, w_pb, w_pc, w_out, final_g)
    y_prompt = _trunk(x_prompt, params, x_prompt.shape[0], x_prompt.shape[1])
    y_sample = _trunk(x_sample, params, x_sample.shape[0], x_sample.shape[1])
    return (y_prompt, y_sample)
```

```python
import functools

import numpy as np
import jax
import jax.numpy as jnp
from jax import lax
from jax.experimental import pallas as pl
from jax.experimental.pallas import tpu as pltpu

F32 = jnp.float32
BF16 = jnp.bfloat16

D_MODEL = 2048
DEPTH = 4
EPS = 1e-6
BLOCK = 128
D_A = 1024
G_A = 4
DG_A = D_A // G_A
HQ_B = 8
HKV_B = 2
HD_B = 128
D_B = HQ_B * HD_B
DKV_B = HKV_B * HD_B
GQ_B = HQ_B // HKV_B
ROPE_THETA = 10000.0
H_C = 4
DK_C = 512
DV_C = 1024
HDK_C = DK_C // H_C
HDV_C = DV_C // H_C
GATE_RANK = 16
GATE_TEMP = 16.0
CHUNK_C = 64
IN_SPLITS = (D_A, D_A, D_A, D_B, DKV_B, DKV_B, D_B, DK_C, DK_C, DV_C, DV_C,
             GATE_RANK, GATE_RANK, D_MODEL, D_MODEL, D_MODEL)
N_IN = sum(IN_SPLITS)
SEG_ORDER = (0, 1, 2, 3, 6, 10, 13, 14, 15, 7, 8, 9, 4, 5, 11, 12)

COL_MIX = 0
MIX_WIDTH = 3 * D_A + 2 * D_B + DV_C
COL_GATE = COL_MIX + MIX_WIDTH
COL_C = COL_GATE + 3 * D_MODEL
COL_KVB = COL_C + 2 * DK_C + DV_C
COL_LR = COL_KVB + 2 * DKV_B
LANES = 128
N_PROJ = 15360
C_WIDTH = 2 * DK_C + DV_C

PREP_TC = 256
PROJ_TM = 2048
PROJ_TN = 1536
GLA_TB = 512
GLA_CUM = 256
GLA_LAG = 9
MM_TM = 2 * BLOCK
MERGE_TN = 2048
NEG_BIG = -1e30
LOG2E = 1.4426950408889634
VMEM_LIMIT = 56 * 1024 * 1024


def _silu(z):
    hz = 0.5 * z
    return hz + hz * jnp.tanh(hz)


def _prep_segments():
    offs = np.concatenate([[0], np.cumsum(IN_SPLITS)])
    segs, dst = [], 0
    for i in SEG_ORDER:
        segs.append((dst, int(offs[i]), IN_SPLITS[i]))
        dst += IN_SPLITS[i]
    return segs, dst


def _prep_w_in_kernel(w_ref, o_ref):
    segs, pad0 = _prep_segments()
    for dst, src, width in segs:
        o_ref[dst:dst + width, :] = w_ref[src:src + width, :].astype(o_ref.dtype)
    o_ref[pad0:, :] = jnp.zeros((N_PROJ - pad0, o_ref.shape[1]), o_ref.dtype)


def _prep_w_in(w_in):
    depth = w_in.shape[0]
    return pl.pallas_call(
        _prep_w_in_kernel,
        grid=(depth, D_MODEL // PREP_TC),
        in_specs=[pl.BlockSpec((None, N_IN, PREP_TC), lambda l, i: (l, 0, i))],
        out_specs=pl.BlockSpec((None, N_PROJ, PREP_TC), lambda l, i: (l, 0, i)),
        out_shape=jax.ShapeDtypeStruct((depth, N_PROJ, D_MODEL), BF16),
        compiler_params=pltpu.CompilerParams(vmem_limit_bytes=VMEM_LIMIT),
        name="prep_w_in",
    )(jnp.swapaxes(w_in, 1, 2))


def _rmsnorm_kernel(x_ref, g_ref, o_ref):
    x = x_ref[...]
    y = x * lax.rsqrt(jnp.mean(x * x, axis=-1, keepdims=True) + EPS) * g_ref[...]
    o_ref[...] = y.astype(o_ref.dtype)


def _rmsnorm(x, g, tm=512):
    t = x.shape[0]
    return pl.pallas_call(
        _rmsnorm_kernel,
        grid=(t // tm,),
        in_specs=[pl.BlockSpec((tm, D_MODEL), lambda i: (i, 0)),
                  pl.BlockSpec((1, D_MODEL), lambda i: (0, 0))],
        out_specs=pl.BlockSpec((tm, D_MODEL), lambda i: (i, 0)),
        out_shape=jax.ShapeDtypeStruct((t, D_MODEL), BF16),
        name="rmsnorm0",
    )(x, g)


def _proj_kernel(h_ref, wt_ref, o_ref):
    o_ref[...] = lax.dot_general(h_ref[...], wt_ref[...], (((1,), (1,)), ((), ())),
                                 preferred_element_type=F32).astype(o_ref.dtype)


def _proj(h, w_all, layer):
    t = h.shape[0]
    return pl.pallas_call(
        _proj_kernel,
        grid=(t // PROJ_TM, N_PROJ // PROJ_TN),
        in_specs=[pl.BlockSpec((PROJ_TM, D_MODEL), lambda i, j: (i, 0)),
                  pl.BlockSpec((None, PROJ_TN, D_MODEL), lambda i, j: (layer, j, 0))],
        out_specs=pl.BlockSpec((PROJ_TM, PROJ_TN), lambda i, j: (i, j)),
        out_shape=jax.ShapeDtypeStruct((t, N_PROJ), BF16),
        compiler_params=pltpu.CompilerParams(vmem_limit_bytes=VMEM_LIMIT),
        name="proj",
    )(h, w_all)


GLA_GROUP = 2 * CHUNK_C


def _cum_matrix(reverse):
    r, c = np.indices((GLA_CUM, GLA_CUM))
    tri = (c >= r) if reverse else (c <= r)
    return jnp.asarray(((r // CHUNK_C == c // CHUNK_C) & tri).astype(np.float32), dtype=BF16)


def _gla_items(c_ref, lr_ref, w_ref, b_ref, cum_ref, o_ref, s_ref, *, reverse):
    nt = (((1,), (1,)), ((), ()))
    tn = (((0,), (0,)), ((), ()))
    last = 0 if reverse else CHUNK_C - 1
    n_groups = GLA_TB // GLA_GROUP
    order = list(range(n_groups - 1, -1, -1) if reverse else range(n_groups))
    env = {}

    def gates():
        x = jnp.dot(lr_ref[...], w_ref[...], preferred_element_type=F32) + b_ref[...]
        g = (jnp.minimum(x, 0.0) - jnp.log(1.0 + jnp.exp(-jnp.abs(x)))) * (1.0 / GATE_TEMP)
        cum = cum_ref[...]
        g_hi = g.astype(BF16)
        g_lo = (g - g_hi.astype(F32)).astype(BF16)
        env["b"] = jnp.concatenate(
            [jnp.dot(cum, g_hi[i:i + GLA_CUM], preferred_element_type=F32)
             + jnp.dot(cum, g_lo[i:i + GLA_CUM], preferred_element_type=F32)
             for i in range(0, GLA_TB, GLA_CUM)], axis=0)
        env["s"] = [s_ref[h] for h in range(H_C)]

    def prep(gi):
        rows0 = slice(gi * GLA_GROUP, gi * GLA_GROUP + CHUNK_C)
        rows1 = slice(gi * GLA_GROUP + CHUNK_C, (gi + 1) * GLA_GROUP)
        b0, b1 = env["b"][rows0], env["b"][rows1]
        bl0, bl1 = b0[last:last + 1], b1[last:last + 1]
        dec0, dec1 = jnp.exp(bl0), jnp.exp(bl1)
        q0 = c_ref[rows0, 0:DK_C].astype(F32) * (HDK_C ** -0.5)
        q1 = c_ref[rows1, 0:DK_C].astype(F32) * (HDK_C ** -0.5)
        k0 = c_ref[rows0, DK_C:2 * DK_C].astype(F32)
        k1 = c_ref[rows1, DK_C:2 * DK_C].astype(F32)
        qe0, qe1 = q0 * jnp.exp(b0), q1 * jnp.exp(b1)
        ke0, ke1 = (k0 * jnp.exp(-b0)).astype(BF16), (k1 * jnp.exp(-b1)).astype(BF16)
        kd0, kd1 = k0 * jnp.exp(bl0 - b0), k1 * jnp.exp(bl1 - b1)
        if reverse:
            q_inter = jnp.concatenate([qe0 * dec1, qe1], axis=0).astype(BF16)
            k_state = jnp.concatenate([kd0, kd1 * dec0], axis=0).astype(BF16)
            keys0 = jnp.concatenate([ke0, kd1.astype(BF16)], axis=0)
            keys1 = jnp.concatenate([ke0, ke1], axis=0)
        else:
            q_inter = jnp.concatenate([qe0, qe1 * dec0], axis=0).astype(BF16)
            k_state = jnp.concatenate([kd0 * dec1, kd1], axis=0).astype(BF16)
            keys0 = jnp.concatenate([ke0, ke1], axis=0)
            keys1 = jnp.concatenate([kd0.astype(BF16), ke1], axis=0)
        env["prep", gi] = (qe0.astype(BF16), qe1.astype(BF16), keys0, keys1, q_inter, k_state,
                           dec0 * dec1)

    def intra(gi):
        qe0, qe1, keys0, keys1, q_inter, k_state, dec = env["prep", gi]
        rows = slice(gi * GLA_GROUP, (gi + 1) * GLA_GROUP)
        r2 = lax.broadcasted_iota(jnp.int32, (GLA_GROUP, GLA_GROUP), 0)
        c2 = lax.broadcasted_iota(jnp.int32, (GLA_GROUP, GLA_GROUP), 1)
        keep = (c2 > r2) if reverse else (c2 <= r2)
        per_head = []
        for h in range(H_C):
            kcols = slice(h * HDK_C, (h + 1) * HDK_C)
            v = c_ref[rows, 2 * DK_C + h * HDV_C:2 * DK_C + (h + 1) * HDV_C]
            a = jnp.concatenate(
                [lax.dot_general(qe0[:, kcols], keys0[:, kcols], nt, preferred_element_type=F32),
                 lax.dot_general(qe1[:, kcols], keys1[:, kcols], nt, preferred_element_type=F32)],
                axis=0)
            a = jnp.where(keep, a, 0.0).astype(BF16)
            ds = lax.dot_general(k_state[:, kcols], v, tn, preferred_element_type=F32)
            dec_col = jnp.transpose(jnp.broadcast_to(dec[:, kcols], (8, HDK_C)))[:, 0:1]
            lhs = jnp.concatenate([a, q_inter[:, kcols]], axis=1)
            per_head.append((lhs, ds, dec_col))
        env["intra", gi] = per_head

    def scan(gi, final):
        rows = slice(gi * GLA_GROUP, (gi + 1) * GLA_GROUP)
        states = env["s"]
        for h, (lhs, ds, dec_col) in enumerate(env["intra", gi]):
            v = c_ref[rows, 2 * DK_C + h * HDV_C:2 * DK_C + (h + 1) * HDV_C]
            rhs = jnp.concatenate([v, states[h].astype(BF16)], axis=0)
            o = jnp.dot(lhs, rhs, preferred_element_type=F32)
            o_ref[rows, h * HDV_C:(h + 1) * HDV_C] = o.astype(o_ref.dtype)
            states[h] = dec_col * states[h] + ds
            if final:
                s_ref[h] = states[h]

    items = [gates]
    for gi in order:
        items += [functools.partial(prep, gi), functools.partial(intra, gi)]
    for n, gi in enumerate(order):
        items.append(functools.partial(scan, gi, n == len(order) - 1))
    return items


def _gla_kernel(cf_ref, lrf_ref, cb_ref, lrb_ref, wf_ref, bf_ref, wb_ref, bb_ref, cumf_ref,
                cumb_ref, of_ref, ob_ref, sf_ref, sb_ref):
    @pl.when(pl.program_id(1) == 0)
    def _():
        sf_ref[...] = jnp.zeros_like(sf_ref)
        sb_ref[...] = jnp.zeros_like(sb_ref)

    fwd = _gla_items(cf_ref, lrf_ref, wf_ref, bf_ref, cumf_ref, of_ref, sf_ref, reverse=False)
    bwd = _gla_items(cb_ref, lrb_ref, wb_ref, bb_ref, cumb_ref, ob_ref, sb_ref, reverse=True)
    lag = GLA_LAG
    for n in range(len(fwd) + lag):
        if n < len(fwd):
            fwd[n]()
        if 0 <= n - lag < len(bwd):
            bwd[n - lag]()


def _gla(proj, wf_pad, bf, wb_pad, bb, batch, seq):
    t = batch * seq
    nblk = seq // GLA_TB

    def fwd(b, j):
        return b * nblk + j

    def bwd(b, j):
        return b * nblk + (nblk - 1 - j)

    cblk = COL_C // C_WIDTH
    lrblk = COL_LR // LANES
    in_specs = [
        pl.BlockSpec((GLA_TB, C_WIDTH), lambda b, j: (fwd(b, j), cblk)),
        pl.BlockSpec((GLA_TB, LANES), lambda b, j: (fwd(b, j), lrblk)),
        pl.BlockSpec((GLA_TB, C_WIDTH), lambda b, j: (bwd(b, j), cblk)),
        pl.BlockSpec((GLA_TB, LANES), lambda b, j: (bwd(b, j), lrblk)),
        pl.BlockSpec((LANES, DK_C), lambda b, j: (0, 0)),
        pl.BlockSpec((1, DK_C), lambda b, j: (0, 0)),
        pl.BlockSpec((LANES, DK_C), lambda b, j: (0, 0)),
        pl.BlockSpec((1, DK_C), lambda b, j: (0, 0)),
        pl.BlockSpec((GLA_CUM, GLA_CUM), lambda b, j: (0, 0)),
        pl.BlockSpec((GLA_CUM, GLA_CUM), lambda b, j: (0, 0)),
    ]
    out_specs = [pl.BlockSpec((GLA_TB, DV_C), lambda b, j: (fwd(b, j), 0)),
                 pl.BlockSpec((GLA_TB, DV_C), lambda b, j: (bwd(b, j), 0))]
    return pl.pallas_call(
        _gla_kernel,
        grid=(batch, nblk),
        in_specs=in_specs,
        out_specs=out_specs,
        out_shape=[jax.ShapeDtypeStruct((t, DV_C), BF16), jax.ShapeDtypeStruct((t, DV_C), BF16)],
        scratch_shapes=[pltpu.VMEM((H_C, HDK_C, HDV_C), F32), pltpu.VMEM((H_C, HDK_C, HDV_C), F32)],
        compiler_params=pltpu.CompilerParams(dimension_semantics=("arbitrary", "arbitrary")),
        name="gla",
    )(proj, proj, proj, proj, wf_pad, bf, wb_pad, bb, _cum_matrix(False), _cum_matrix(True))


def _mix_a_items(a3_ref, lng_ref, lnb_ref, ws_ref, bias_ref, ya_ref):
    env = {}

    def norm(blk):
        rows = slice(blk * BLOCK, (blk + 1) * BLOCK)
        va = a3_ref[rows, D_A:2 * D_A].astype(F32)
        mu = jnp.mean(va, axis=-1, keepdims=True)
        dv = va - mu
        var = jnp.mean(dv * dv, axis=-1, keepdims=True)
        env[blk] = (dv * lax.rsqrt(var + EPS) * lng_ref[...] + lnb_ref[...]).astype(BF16)

    def group(blk, g):
        rows = slice(blk * BLOCK, (blk + 1) * BLOCK)
        cols = slice(g * DG_A, (g + 1) * DG_A)
        f = jnp.dot(ws_ref[g], env[blk][:, cols], preferred_element_type=F32) + bias_ref[:, cols]
        ua = a3_ref[rows, g * DG_A:(g + 1) * DG_A].astype(F32)
        za = a3_ref[rows, 2 * D_A + g * DG_A:2 * D_A + (g + 1) * DG_A].astype(F32)
        ya_ref[rows, cols] = (ua * f * _silu(za)).astype(ya_ref.dtype)

    items = []
    for blk in range(MM_TM // BLOCK):
        items.append(functools.partial(norm, blk))
        items += [functools.partial(group, blk, g) for g in range(G_A)]
    return items


def _mix_b_items(sink_ref, q_ref, zb_ref, kvp_ref, kvc_ref, kvn_ref, csp_ref, csc_ref, csn_ref,
                 yb_ref, first, final):
    def rope(x, cs):
        return x * cs[:, :HD_B] + pltpu.roll(x, HD_B // 2, 1) * cs[:, HD_B:]

    q_scale = (HD_B ** -0.5) * LOG2E
    roped = {}
    env = {}

    def logits(kh, blk):
        kcol = slice(kh * HD_B, (kh + 1) * HD_B)
        vcol = slice(DKV_B + kh * HD_B, DKV_B + (kh + 1) * HD_B)
        if kh not in roped:
            cs_k = jnp.concatenate([csp_ref[...], csc_ref[...], csn_ref[...]], axis=0)
            k_raw = jnp.concatenate([kvp_ref[:, kcol], kvc_ref[:, kcol], kvn_ref[:, kcol]], axis=0)
            k4 = rope(k_raw.astype(F32), cs_k).astype(BF16)
            v4 = jnp.concatenate([kvp_ref[:, vcol], kvc_ref[:, vcol], kvn_ref[:, vcol]], axis=0)
            roped[kh] = (k4, v4)
        k4, v4 = roped[kh]
        rows = slice(blk * BLOCK, (blk + 1) * BLOCK)
        if ("mask", blk) not in env:
            il = lax.broadcasted_iota(jnp.int32, (BLOCK, BLOCK), 0)
            jl = lax.broadcasted_iota(jnp.int32, (BLOCK, BLOCK), 1)
            has_prev = 1 if blk > 0 else 1 - first
            has_next = 1 if blk < MM_TM // BLOCK - 1 else 1 - final
            env["mask", blk] = (jnp.where(jl >= il + BLOCK * (1 - has_prev), 0.0, NEG_BIG),
                                jnp.where(jl <= il - BLOCK * (1 - has_next), 0.0, NEG_BIG))
        k3 = k4[blk * BLOCK:(blk + 3) * BLOCK]
        v3 = v4[blk * BLOCK:(blk + 3) * BLOCK]
        cs_q = csc_ref[rows, :]
        qs = jnp.concatenate(
            [rope(q_ref[rows, (kh * GQ_B + g) * HD_B:(kh * GQ_B + g + 1) * HD_B].astype(F32),
                  cs_q) * q_scale for g in range(GQ_B)], axis=0).astype(BF16)
        s = lax.dot_general(qs, k3, (((1,), (1,)), ((), ())), preferred_element_type=F32)
        env[kh, blk] = (s, v3)

    def head(kh, blk, g):
        s, v3 = env[kh, blk]
        mask_prev, mask_next = env["mask", blk]
        rows = slice(blk * BLOCK, (blk + 1) * BLOCK)
        hq = kh * GQ_B + g
        sink = sink_ref[hq] * LOG2E
        sg = s[g * BLOCK:(g + 1) * BLOCK]
        sg = jnp.concatenate([sg[:, :BLOCK] + mask_prev, sg[:, BLOCK:2 * BLOCK],
                              sg[:, 2 * BLOCK:] + mask_next], axis=1)
        m = jnp.maximum(jnp.max(sg, axis=-1, keepdims=True), sink)
        p = jnp.exp2(sg - m)
        den = jnp.sum(p, axis=-1, keepdims=True) + jnp.exp2(sink - m)
        o = jnp.dot(p.astype(BF16), v3, preferred_element_type=F32) / den
        hc = slice(hq * HD_B, (hq + 1) * HD_B)
        yb_ref[rows, hc] = (o * _silu(zb_ref[rows, hc].astype(F32))).astype(yb_ref.dtype)

    items = []
    for kh in range(HKV_B):
        for blk in range(MM_TM // BLOCK):
            items.append(functools.partial(logits, kh, blk))
            items += [functools.partial(head, kh, blk, g) for g in range(GQ_B)]
    return items


def _mix_c_items(of_ref, ob_ref, zc_ref, cg_ref, yc_ref):
    def item(h):
        hc = slice(h * HDV_C, (h + 1) * HDV_C)
        oh = of_ref[:, hc].astype(F32) + ob_ref[:, hc].astype(F32)
        ms = jnp.mean(oh * oh, axis=-1, keepdims=True)
        yc_ref[:, hc] = (oh * lax.rsqrt(ms + EPS) * cg_ref[...]
                         * _silu(zc_ref[:, hc].astype(F32))).astype(yc_ref.dtype)

    return [functools.partial(item, h) for h in range(H_C)]


def _merge_items(ya_ref, yb_ref, yc_ref, ga_ref, gb_ref, gc_ref, x_ref,
                 wpa_ref, wpb_ref, wpc_ref, wout_ref, ng_ref, x_out_ref, h_ref,
                 mg_s, xn_s, ss_s):
    n_chunks = D_MODEL // MERGE_TN

    branches = ((ya_ref, wpa_ref, ga_ref), (yb_ref, wpb_ref, gb_ref), (yc_ref, wpc_ref, gc_ref))
    env = {}

    def branch(c, k):
        cols = slice(c * MERGE_TN, (c + 1) * MERGE_TN)
        y_ref, w_ref, g_ref = branches[k]
        part = (jax.nn.sigmoid(g_ref[:, cols].astype(F32))
                * jnp.dot(y_ref[...], w_ref[:, cols], preferred_element_type=F32))
        acc = part if k == 0 else env[c] + part
        if k == len(branches) - 1:
            mg_s[:, cols] = acc.astype(mg_s.dtype)
        else:
            env[c] = acc

    def out(c):
        cols = slice(c * MERGE_TN, (c + 1) * MERGE_TN)
        xn = x_ref[:, cols] + jnp.dot(mg_s[...], wout_ref[:, cols], preferred_element_type=F32)
        xn_s[:, cols] = xn
        if x_out_ref is not None:
            x_out_ref[:, cols] = xn
        part = jnp.sum(xn * xn, axis=-1, keepdims=True)
        ss_s[...] = part if c == 0 else ss_s[...] + part

    def norm():
        hn = xn_s[...] * lax.rsqrt(ss_s[...] * (1.0 / D_MODEL) + EPS) * ng_ref[...]
        h_ref[...] = hn.astype(h_ref.dtype)

    return ([functools.partial(branch, c, k) for c in range(n_chunks) for k in range(len(branches))]
            + [functools.partial(out, c) for c in range(n_chunks)] + [norm])


def _mixmerge_kernel(sink_ref, m_ref, kvp_ref, kvc_ref, kvn_ref,
                     csp_ref, csc_ref, csn_ref, lng_ref, lnb_ref, ws_ref, bias_ref,
                     of_ref, ob_ref, g_ref, x_ref,
                     wpa_ref, wpb_ref, wpc_ref, wout_ref, cg_ref, ng_ref,
                     *rest, n_tiles, tiles_per_seq, emit_x):
    if emit_x:
        x_out_ref, h_ref, ya_s, yb_s, yc_s, mg_s, xn_s, ss_s = rest
    else:
        x_out_ref = None
        h_ref, ya_s, yb_s, yc_s, mg_s, xn_s, ss_s = rest
    a3_ref = m_ref.at[:, 0:3 * D_A]
    q_ref = m_ref.at[:, 3 * D_A:3 * D_A + D_B]
    zb_ref = m_ref.at[:, 3 * D_A + D_B:3 * D_A + 2 * D_B]
    zc_ref = m_ref.at[:, 3 * D_A + 2 * D_B:MIX_WIDTH]
    ga_ref, gb_ref, gc_ref = [g_ref.at[:, k * D_MODEL:(k + 1) * D_MODEL] for k in range(3)]
    r = pl.program_id(0)
    slot = lax.rem(r, 2)

    @pl.when(r == 0)
    def _():
        ya_s[1] = jnp.zeros(ya_s.shape[1:], ya_s.dtype)
        yb_s[1] = jnp.zeros(yb_s.shape[1:], yb_s.dtype)
        yc_s[1] = jnp.zeros(yc_s.shape[1:], yc_s.dtype)

    pos = lax.rem(jnp.minimum(r, n_tiles - 1), tiles_per_seq)
    first = (pos == 0).astype(jnp.int32)
    final = (pos == tiles_per_seq - 1).astype(jnp.int32)
    mix_a = _mix_a_items(a3_ref, lng_ref, lnb_ref, ws_ref, bias_ref, ya_s.at[slot])
    mix_b = _mix_b_items(sink_ref, q_ref, zb_ref, kvp_ref, kvc_ref, kvn_ref, csp_ref, csc_ref,
                         csn_ref, yb_s.at[slot], first, final)
    mix_c = _mix_c_items(of_ref, ob_ref, zc_ref, cg_ref, yc_s.at[slot])
    merge = _merge_items(ya_s.at[1 - slot], yb_s.at[1 - slot], yc_s.at[1 - slot], ga_ref, gb_ref,
                         gc_ref, x_ref, wpa_ref, wpb_ref, wpc_ref, wout_ref, ng_ref, x_out_ref,
                         h_ref, mg_s, xn_s, ss_s)
    mix = mix_c[:2] + mix_b[:10] + mix_a[:5] + mix_c[2:] + mix_b[10:] + mix_a[5:]
    slots = len(merge) - 1
    done = 0
    for n, merge_item in enumerate(merge):
        merge_item()
        upto = len(mix) if n >= slots - 1 else (len(mix) * (n + 1)) // slots
        for mix_item in mix[done:upto]:
            mix_item()
        done = upto


def _mixmerge(proj, o_f, o_b, x, cs, params, layer, next_g, batch, seq, last):
    t = batch * seq
    n_tiles = t // MM_TM
    tps = seq // MM_TM
    nb2 = MM_TM // BLOCK

    def mix_tile(r):
        return jnp.minimum(r, n_tiles - 1)

    def merge_tile(r):
        return jnp.maximum(r - 1, 0)

    def kv_prev(r):
        m = mix_tile(r)
        return jnp.where(lax.rem(m, tps) == 0, nb2 * m, nb2 * m - 1)

    def kv_next(r):
        m = mix_tile(r)
        return jnp.where(lax.rem(m, tps) == tps - 1, nb2 * m + nb2 - 1, nb2 * m + nb2)

    def cs_prev(r):
        i = lax.rem(mix_tile(r), tps)
        return jnp.maximum(nb2 * i - 1, 0)

    def cs_next(r):
        i = lax.rem(mix_tile(r), tps)
        return jnp.minimum(nb2 * i + nb2, nb2 * tps - 1)

    mix = lambda width, blk: pl.BlockSpec((MM_TM, width), lambda r: (mix_tile(r), blk))
    mrg = lambda width, blk: pl.BlockSpec((MM_TM, width), lambda r: (merge_tile(r), blk))
    per_layer = lambda shape: pl.BlockSpec((None,) + shape, lambda r: (layer,) + (0,) * len(shape))
    weight = lambda shape: pl.BlockSpec((None,) + shape, lambda r: (layer,) + (0,) * len(shape),
                                        pipeline_mode=pl.Buffered(1))
    kvb = COL_KVB // (2 * DKV_B)
    in_specs = [
        pl.BlockSpec(memory_space=pltpu.SMEM),
        mix(MIX_WIDTH, COL_MIX // MIX_WIDTH),
        pl.BlockSpec((BLOCK, 2 * DKV_B), lambda r: (kv_prev(r), kvb)),
        mix(2 * DKV_B, kvb),
        pl.BlockSpec((BLOCK, 2 * DKV_B), lambda r: (kv_next(r), kvb)),
        pl.BlockSpec((BLOCK, 2 * HD_B), lambda r: (cs_prev(r), 0)),
        pl.BlockSpec((MM_TM, 2 * HD_B), lambda r: (lax.rem(mix_tile(r), tps), 0)),
        pl.BlockSpec((BLOCK, 2 * HD_B), lambda r: (cs_next(r), 0)),
        per_layer((1, D_A)), per_layer((1, D_A)), per_layer((G_A, BLOCK, BLOCK)),
        per_layer((BLOCK, D_A)),
        mix(DV_C, 0), mix(DV_C, 0),
        mrg(3 * D_MODEL, COL_GATE // (3 * D_MODEL)),
        mrg(D_MODEL, 0),
        weight((D_A, D_MODEL)), weight((D_B, D_MODEL)), weight((DV_C, D_MODEL)),
        weight((D_MODEL, D_MODEL)), per_layer((1, HDV_C)),
        pl.BlockSpec((1, D_MODEL), lambda r: (0, 0)),
    ]
    h_dtype = F32 if last else BF16
    out_specs = [mrg(D_MODEL, 0)]
    out_shape = [jax.ShapeDtypeStruct((t, D_MODEL), h_dtype)]
    if not last:
        out_specs = [mrg(D_MODEL, 0)] + out_specs
        out_shape = [jax.ShapeDtypeStruct((t, D_MODEL), F32)] + out_shape
    outs = pl.pallas_call(
        functools.partial(_mixmerge_kernel, n_tiles=n_tiles, tiles_per_seq=tps, emit_x=not last),
        grid=(n_tiles + 1,),
        in_specs=in_specs,
        out_specs=out_specs,
        out_shape=out_shape,
        scratch_shapes=[pltpu.VMEM((2, MM_TM, D_A), BF16), pltpu.VMEM((2, MM_TM, D_B), BF16),
                        pltpu.VMEM((2, MM_TM, DV_C), BF16), pltpu.VMEM((MM_TM, D_MODEL), BF16),
                        pltpu.VMEM((MM_TM, D_MODEL), F32), pltpu.VMEM((MM_TM, 1), F32)],
        compiler_params=pltpu.CompilerParams(dimension_semantics=("arbitrary",),
                                             vmem_limit_bytes=VMEM_LIMIT),
        name="mixmerge",
    )(params["b_sink"][layer], proj, proj, proj, proj, cs, cs, cs,
      params["a_ln_g"], params["a_ln_b"], params["a_ws"], params["a_bias"],
      o_f, o_b, proj, x,
      params["w_pa"], params["w_pb"], params["w_pc"], params["w_out"], params["c_norm_g"], next_g)
    return (None, outs[0]) if last else (outs[0], outs[1])


def _rope_table(seq):
    half = HD_B // 2
    inv = ROPE_THETA ** (-jnp.arange(half, dtype=F32) * 2.0 / HD_B)
    ang = jnp.arange(seq, dtype=F32)[:, None] * inv[None, :]
    cos, sin = jnp.cos(ang), jnp.sin(ang)
    return jnp.concatenate([cos, cos, -sin, sin], axis=-1)


def _pad_gate_w(w, row0):
    z = jnp.zeros((w.shape[0], LANES, DK_C), w.dtype)
    return z.at[:, row0:row0 + GATE_RANK].set(w).astype(BF16)


def _prepare(norm_g, w_in, a_ln_g, a_ln_b, a_ws, a_bs, b_sink, c_wf, c_bf, c_wb, c_bb, c_norm_g,
             w_pa, w_pb, w_pc, w_out, final_g):
    return dict(
        norm_g=norm_g, final_g=final_g, w_in=_prep_w_in(w_in),
        a_ln_g=a_ln_g[:, None], a_ln_b=a_ln_b[:, None], a_ws=a_ws.astype(BF16),
        a_bias=jnp.repeat(jnp.swapaxes(a_bs, 1, 2), DG_A, axis=2),
        b_sink=b_sink,
        c_wf=_pad_gate_w(c_wf, 0), c_bf=c_bf, c_wb=_pad_gate_w(c_wb, GATE_RANK), c_bb=c_bb,
        c_norm_g=c_norm_g[:, None], w_pa=w_pa.astype(BF16), w_pb=w_pb.astype(BF16),
        w_pc=w_pc.astype(BF16), w_out=w_out.astype(BF16),
    )


def _trunk(x, params, batch, seq):
    t = batch * seq
    x = x.reshape(t, D_MODEL)
    cs = _rope_table(seq)
    h = _rmsnorm(x, params["norm_g"][0][None])
    for l in range(DEPTH):
        last = l == DEPTH - 1
        proj = _proj(h, params["w_in"], l)
        o_f, o_b = _gla(proj, params["c_wf"][l], params["c_bf"][l][None], params["c_wb"][l],
                        params["c_bb"][l][None], batch, seq)
        next_g = params["final_g"] if last else params["norm_g"][l + 1]
        x, h = _mixmerge(proj, o_f, o_b, x, cs, params, l, next_g[None], batch, seq, last)
    return h.reshape(batch, seq, D_MODEL)


def kernel(x_prompt, x_sample, norm_g, w_in, a_ln_g, a_ln_b, a_ws, a_bs, b_sink, c_wf, c_bf,
           c_wb, c_bb, c_norm_g, w_pa, w_pb, w_pc, w_out, final_g):
    params = _prepare(norm_g, w_in, a_ln_g, a_ln_b, a_ws, a_bs, b_sink, c_wf, c_bf, c_wb, c_bb,
                      c_norm_g, w_pa, w_pb, w_pc, w_out, final_g)
    y_prompt = _trunk(x_prompt, params, x_prompt.shape[0], x_prompt.shape[1])
    y_sample = _trunk(x_sample, params, x_sample.shape[0], x_sample.shape[1])
    return (y_prompt, y_sample)
```

```python
import functools

import numpy as np
import jax
import jax.numpy as jnp
from jax import lax
from jax.experimental import pallas as pl
from jax.experimental.pallas import tpu as pltpu

F32 = jnp.float32
BF16 = jnp.bfloat16

D_MODEL = 2048
DEPTH = 4
EPS = 1e-6
BLOCK = 128
D_A = 1024
G_A = 4
DG_A = D_A // G_A
HQ_B = 8
HKV_B = 2
HD_B = 128
D_B = HQ_B * HD_B
DKV_B = HKV_B * HD_B
GQ_B = HQ_B // HKV_B
ROPE_THETA = 10000.0
H_C = 4
DK_C = 512
DV_C = 1024
HDK_C = DK_C // H_C
HDV_C = DV_C // H_C
GATE_RANK = 16
GATE_TEMP = 16.0
CHUNK_C = 64
IN_SPLITS = (D_A, D_A, D_A, D_B, DKV_B, DKV_B, D_B, DK_C, DK_C, DV_C, DV_C,
             GATE_RANK, GATE_RANK, D_MODEL, D_MODEL, D_MODEL)
N_IN = sum(IN_SPLITS)
SEG_ORDER = (0, 1, 2, 3, 6, 10, 13, 14, 15, 7, 8, 9, 4, 5, 11, 12)
HALVED_SEGS = (2, 6, 10, 13, 14, 15)

COL_MIX = 0
MIX_WIDTH = 3 * D_A + 2 * D_B + DV_C
COL_GATE = COL_MIX + MIX_WIDTH
COL_C = COL_GATE + 3 * D_MODEL
COL_KVB = COL_C + 2 * DK_C + DV_C
COL_LR = COL_KVB + 2 * DKV_B
LANES = 128
N_PROJ = 15360
C_WIDTH = 2 * DK_C + DV_C

PREP_TC = 256
PROJ_TM = 2048
PROJ_TN = 1536
GLA_TB = 512
GLA_CUM = 256
GLA_LAG = 9
MM_TM = 2 * BLOCK
MERGE_TN = 2048
NEG_BIG = -1e30
LOG2E = 1.4426950408889634
VMEM_LIMIT = 56 * 1024 * 1024


def _silu_of_half(hz):
    return hz + hz * jnp.tanh(hz)


def _prep_segments():
    offs = np.concatenate([[0], np.cumsum(IN_SPLITS)])
    segs, dst = [], 0
    for i in SEG_ORDER:
        segs.append((dst, int(offs[i]), IN_SPLITS[i], 0.5 if i in HALVED_SEGS else 1.0))
        dst += IN_SPLITS[i]
    return segs, dst


def _prep_w_in_kernel(w_ref, o_ref):
    segs, pad0 = _prep_segments()
    for dst, src, width, scale in segs:
        w = w_ref[src:src + width, :]
        if scale != 1.0:
            w = w * scale
        o_ref[dst:dst + width, :] = w.astype(o_ref.dtype)
    o_ref[pad0:, :] = jnp.zeros((N_PROJ - pad0, o_ref.shape[1]), o_ref.dtype)


def _prep_w_in(w_in):
    depth = w_in.shape[0]
    return pl.pallas_call(
        _prep_w_in_kernel,
        grid=(depth, D_MODEL // PREP_TC),
        in_specs=[pl.BlockSpec((None, N_IN, PREP_TC), lambda l, i: (l, 0, i))],
        out_specs=pl.BlockSpec((None, N_PROJ, PREP_TC), lambda l, i: (l, 0, i)),
        out_shape=jax.ShapeDtypeStruct((depth, N_PROJ, D_MODEL), BF16),
        compiler_params=pltpu.CompilerParams(vmem_limit_bytes=VMEM_LIMIT),
        name="prep_w_in",
    )(jnp.swapaxes(w_in, 1, 2))


def _rmsnorm_kernel(x_ref, g_ref, o_ref):
    x = x_ref[...]
    y = x * lax.rsqrt(jnp.mean(x * x, axis=-1, keepdims=True) + EPS) * g_ref[...]
    o_ref[...] = y.astype(o_ref.dtype)


def _rmsnorm(x, g, tm=512):
    t = x.shape[0]
    return pl.pallas_call(
        _rmsnorm_kernel,
        grid=(t // tm,),
        in_specs=[pl.BlockSpec((tm, D_MODEL), lambda i: (i, 0)),
                  pl.BlockSpec((1, D_MODEL), lambda i: (0, 0))],
        out_specs=pl.BlockSpec((tm, D_MODEL), lambda i: (i, 0)),
        out_shape=jax.ShapeDtypeStruct((t, D_MODEL), BF16),
        name="rmsnorm0",
    )(x, g)


def _proj_kernel(h_ref, wt_ref, o_ref):
    o_ref[...] = lax.dot_general(h_ref[...], wt_ref[...], (((1,), (1,)), ((), ())),
                                 preferred_element_type=F32).astype(o_ref.dtype)


def _proj(h, w_all, layer):
    t = h.shape[0]
    return pl.pallas_call(
        _proj_kernel,
        grid=(t // PROJ_TM, N_PROJ // PROJ_TN),
        in_specs=[pl.BlockSpec((PROJ_TM, D_MODEL), lambda i, j: (i, 0)),
                  pl.BlockSpec((None, PROJ_TN, D_MODEL), lambda i, j: (layer, j, 0))],
        out_specs=pl.BlockSpec((PROJ_TM, PROJ_TN), lambda i, j: (i, j)),
        out_shape=jax.ShapeDtypeStruct((t, N_PROJ), BF16),
        compiler_params=pltpu.CompilerParams(vmem_limit_bytes=VMEM_LIMIT),
        name="proj",
    )(h, w_all)


GLA_GROUP = 2 * CHUNK_C


def _cum_matrix(reverse):
    r, c = np.indices((GLA_CUM, GLA_CUM))
    tri = (c >= r) if reverse else (c <= r)
    return jnp.asarray(((r // CHUNK_C == c // CHUNK_C) & tri).astype(np.float32), dtype=BF16)


def _gla_items(c_ref, lr_ref, w_ref, b_ref, cum_ref, o_ref, s_ref, *, reverse):
    nt = (((1,), (1,)), ((), ()))
    tn = (((0,), (0,)), ((), ()))
    last = 0 if reverse else CHUNK_C - 1
    n_groups = GLA_TB // GLA_GROUP
    order = list(range(n_groups - 1, -1, -1) if reverse else range(n_groups))
    env = {}

    def gates():
        x = jnp.dot(lr_ref[...], w_ref[...], preferred_element_type=F32) + b_ref[...]
        g = (jnp.minimum(x, 0.0) * (LOG2E / GATE_TEMP)
             - jnp.log2(1.0 + jnp.exp2(jnp.abs(x) * -LOG2E)) * (1.0 / GATE_TEMP))
        cum = cum_ref[...]
        g_hi = g.astype(BF16)
        g_lo = (g - g_hi.astype(F32)).astype(BF16)
        env["b"] = jnp.concatenate(
            [jnp.dot(cum, g_hi[i:i + GLA_CUM], preferred_element_type=F32)
             + jnp.dot(cum, g_lo[i:i + GLA_CUM], preferred_element_type=F32)
             for i in range(0, GLA_TB, GLA_CUM)], axis=0)
        env["s"] = [s_ref[h] for h in range(H_C)]

    def prep(gi):
        rows0 = slice(gi * GLA_GROUP, gi * GLA_GROUP + CHUNK_C)
        rows1 = slice(gi * GLA_GROUP + CHUNK_C, (gi + 1) * GLA_GROUP)
        b0, b1 = env["b"][rows0], env["b"][rows1]
        bl0, bl1 = b0[last:last + 1], b1[last:last + 1]
        dec0, dec1 = jnp.exp2(bl0), jnp.exp2(bl1)
        q0 = c_ref[rows0, 0:DK_C].astype(F32) * (HDK_C ** -0.5)
        q1 = c_ref[rows1, 0:DK_C].astype(F32) * (HDK_C ** -0.5)
        k0 = c_ref[rows0, DK_C:2 * DK_C].astype(F32)
        k1 = c_ref[rows1, DK_C:2 * DK_C].astype(F32)
        qe0, qe1 = q0 * jnp.exp2(b0), q1 * jnp.exp2(b1)
        ke0, ke1 = (k0 * jnp.exp2(-b0)).astype(BF16), (k1 * jnp.exp2(-b1)).astype(BF16)
        kd0, kd1 = k0 * jnp.exp2(bl0 - b0), k1 * jnp.exp2(bl1 - b1)
        if reverse:
            q_inter = jnp.concatenate([qe0 * dec1, qe1], axis=0).astype(BF16)
            k_state = jnp.concatenate([kd0, kd1 * dec0], axis=0).astype(BF16)
            keys0 = jnp.concatenate([ke0, kd1.astype(BF16)], axis=0)
            keys1 = jnp.concatenate([ke0, ke1], axis=0)
        else:
            q_inter = jnp.concatenate([qe0, qe1 * dec0], axis=0).astype(BF16)
            k_state = jnp.concatenate([kd0 * dec1, kd1], axis=0).astype(BF16)
            keys0 = jnp.concatenate([ke0, ke1], axis=0)
            keys1 = jnp.concatenate([kd0.astype(BF16), ke1], axis=0)
        env["prep", gi] = (qe0.astype(BF16), qe1.astype(BF16), keys0, keys1, q_inter, k_state,
                           dec0 * dec1)

    def intra(gi):
        qe0, qe1, keys0, keys1, q_inter, k_state, dec = env["prep", gi]
        rows = slice(gi * GLA_GROUP, (gi + 1) * GLA_GROUP)
        r2 = lax.broadcasted_iota(jnp.int32, (GLA_GROUP, GLA_GROUP), 0)
        c2 = lax.broadcasted_iota(jnp.int32, (GLA_GROUP, GLA_GROUP), 1)
        keep = (c2 > r2) if reverse else (c2 <= r2)
        per_head = []
        for h in range(H_C):
            kcols = slice(h * HDK_C, (h + 1) * HDK_C)
            v = c_ref[rows, 2 * DK_C + h * HDV_C:2 * DK_C + (h + 1) * HDV_C]
            a = jnp.concatenate(
                [lax.dot_general(qe0[:, kcols], keys0[:, kcols], nt, preferred_element_type=F32),
                 lax.dot_general(qe1[:, kcols], keys1[:, kcols], nt, preferred_element_type=F32)],
                axis=0)
            a = jnp.where(keep, a, 0.0).astype(BF16)
            ds = lax.dot_general(k_state[:, kcols], v, tn, preferred_element_type=F32)
            dec_col = jnp.transpose(jnp.broadcast_to(dec[:, kcols], (8, HDK_C)))[:, 0:1]
            lhs = jnp.concatenate([a, q_inter[:, kcols]], axis=1)
            per_head.append((lhs, ds, dec_col))
        env["intra", gi] = per_head

    def scan(gi, final):
        rows = slice(gi * GLA_GROUP, (gi + 1) * GLA_GROUP)
        states = env["s"]
        for h, (lhs, ds, dec_col) in enumerate(env["intra", gi]):
            v = c_ref[rows, 2 * DK_C + h * HDV_C:2 * DK_C + (h + 1) * HDV_C]
            rhs = jnp.concatenate([v, states[h].astype(BF16)], axis=0)
            o = jnp.dot(lhs, rhs, preferred_element_type=F32)
            o_ref[rows, h * HDV_C:(h + 1) * HDV_C] = o.astype(o_ref.dtype)
            states[h] = dec_col * states[h] + ds
            if final:
                s_ref[h] = states[h]

    items = [gates]
    for gi in order:
        items += [functools.partial(prep, gi), functools.partial(intra, gi)]
    for n, gi in enumerate(order):
        items.append(functools.partial(scan, gi, n == len(order) - 1))
    return items


def _gla_kernel(cf_ref, lrf_ref, cb_ref, lrb_ref, wf_ref, bf_ref, wb_ref, bb_ref, cumf_ref,
                cumb_ref, of_ref, ob_ref, sf_ref, sb_ref):
    @pl.when(pl.program_id(1) == 0)
    def _():
        sf_ref[...] = jnp.zeros_like(sf_ref)
        sb_ref[...] = jnp.zeros_like(sb_ref)

    fwd = _gla_items(cf_ref, lrf_ref, wf_ref, bf_ref, cumf_ref, of_ref, sf_ref, reverse=False)
    bwd = _gla_items(cb_ref, lrb_ref, wb_ref, bb_ref, cumb_ref, ob_ref, sb_ref, reverse=True)
    lag = GLA_LAG
    for n in range(len(fwd) + lag):
        if n < len(fwd):
            fwd[n]()
        if 0 <= n - lag < len(bwd):
            bwd[n - lag]()


def _gla(proj, wf_pad, bf, wb_pad, bb, batch, seq):
    t = batch * seq
    nblk = seq // GLA_TB

    def fwd(b, j):
        return b * nblk + j

    def bwd(b, j):
        return b * nblk + (nblk - 1 - j)

    cblk = COL_C // C_WIDTH
    lrblk = COL_LR // LANES
    in_specs = [
        pl.BlockSpec((GLA_TB, C_WIDTH), lambda b, j: (fwd(b, j), cblk)),
        pl.BlockSpec((GLA_TB, LANES), lambda b, j: (fwd(b, j), lrblk)),
        pl.BlockSpec((GLA_TB, C_WIDTH), lambda b, j: (bwd(b, j), cblk)),
        pl.BlockSpec((GLA_TB, LANES), lambda b, j: (bwd(b, j), lrblk)),
        pl.BlockSpec((LANES, DK_C), lambda b, j: (0, 0)),
        pl.BlockSpec((1, DK_C), lambda b, j: (0, 0)),
        pl.BlockSpec((LANES, DK_C), lambda b, j: (0, 0)),
        pl.BlockSpec((1, DK_C), lambda b, j: (0, 0)),
        pl.BlockSpec((GLA_CUM, GLA_CUM), lambda b, j: (0, 0)),
        pl.BlockSpec((GLA_CUM, GLA_CUM), lambda b, j: (0, 0)),
    ]
    out_specs = [pl.BlockSpec((GLA_TB, DV_C), lambda b, j: (fwd(b, j), 0)),
                 pl.BlockSpec((GLA_TB, DV_C), lambda b, j: (bwd(b, j), 0))]
    return pl.pallas_call(
        _gla_kernel,
        grid=(batch, nblk),
        in_specs=in_specs,
        out_specs=out_specs,
        out_shape=[jax.ShapeDtypeStruct((t, DV_C), BF16), jax.ShapeDtypeStruct((t, DV_C), BF16)],
        scratch_shapes=[pltpu.VMEM((H_C, HDK_C, HDV_C), F32), pltpu.VMEM((H_C, HDK_C, HDV_C), F32)],
        compiler_params=pltpu.CompilerParams(dimension_semantics=("arbitrary", "arbitrary")),
        name="gla",
    )(proj, proj, proj, proj, wf_pad, bf, wb_pad, bb, _cum_matrix(False), _cum_matrix(True))


def _mix_a_items(a3_ref, lng_ref, lnb_ref, ws_ref, bias_ref, ya_ref):
    env = {}

    def norm(blk):
        rows = slice(blk * BLOCK, (blk + 1) * BLOCK)
        va = a3_ref[rows, D_A:2 * D_A].astype(F32)
        mu = jnp.mean(va, axis=-1, keepdims=True)
        dv = va - mu
        var = jnp.mean(dv * dv, axis=-1, keepdims=True)
        env[blk] = (dv * lax.rsqrt(var + EPS) * lng_ref[...] + lnb_ref[...]).astype(BF16)

    def group(blk, g):
        rows = slice(blk * BLOCK, (blk + 1) * BLOCK)
        cols = slice(g * DG_A, (g + 1) * DG_A)
        f = jnp.dot(ws_ref[g], env[blk][:, cols], preferred_element_type=F32) + bias_ref[:, cols]
        ua = a3_ref[rows, g * DG_A:(g + 1) * DG_A].astype(F32)
        za = a3_ref[rows, 2 * D_A + g * DG_A:2 * D_A + (g + 1) * DG_A].astype(F32)
        ya_ref[rows, cols] = (ua * f * _silu_of_half(za)).astype(ya_ref.dtype)

    items = []
    for blk in range(MM_TM // BLOCK):
        items.append(functools.partial(norm, blk))
        items += [functools.partial(group, blk, g) for g in range(G_A)]
    return items


def _mix_b_items(sink_ref, q_ref, zb_ref, kvp_ref, kvc_ref, kvn_ref, csp_ref, csc_ref, csn_ref,
                 yb_ref, first, final):
    def rope(x, cs):
        return x * cs[:, :HD_B] + pltpu.roll(x, HD_B // 2, 1) * cs[:, HD_B:]

    q_scale = (HD_B ** -0.5) * LOG2E
    roped = {}
    env = {}

    def logits(kh, blk):
        kcol = slice(kh * HD_B, (kh + 1) * HD_B)
        vcol = slice(DKV_B + kh * HD_B, DKV_B + (kh + 1) * HD_B)
        if kh not in roped:
            cs_k = jnp.concatenate([csp_ref[...], csc_ref[...], csn_ref[...]], axis=0)
            k_raw = jnp.concatenate([kvp_ref[:, kcol], kvc_ref[:, kcol], kvn_ref[:, kcol]], axis=0)
            k4 = rope(k_raw.astype(F32), cs_k).astype(BF16)
            v4 = jnp.concatenate([kvp_ref[:, vcol], kvc_ref[:, vcol], kvn_ref[:, vcol]], axis=0)
            roped[kh] = (k4, v4)
        k4, v4 = roped[kh]
        rows = slice(blk * BLOCK, (blk + 1) * BLOCK)
        if ("mask", blk) not in env:
            il = lax.broadcasted_iota(jnp.int32, (BLOCK, BLOCK), 0)
            jl = lax.broadcasted_iota(jnp.int32, (BLOCK, BLOCK), 1)
            has_prev = 1 if blk > 0 else 1 - first
            has_next = 1 if blk < MM_TM // BLOCK - 1 else 1 - final
            env["mask", blk] = (jnp.where(jl >= il + BLOCK * (1 - has_prev), 0.0, NEG_BIG),
                                jnp.where(jl <= il - BLOCK * (1 - has_next), 0.0, NEG_BIG))
        k3 = k4[blk * BLOCK:(blk + 3) * BLOCK]
        v3 = v4[blk * BLOCK:(blk + 3) * BLOCK]
        cs_q = csc_ref[rows, :]
        qs = jnp.concatenate(
            [rope(q_ref[rows, (kh * GQ_B + g) * HD_B:(kh * GQ_B + g + 1) * HD_B].astype(F32),
                  cs_q) * q_scale for g in range(GQ_B)], axis=0).astype(BF16)
        s = lax.dot_general(qs, k3, (((1,), (1,)), ((), ())), preferred_element_type=F32)
        env[kh, blk] = (s, v3)

    def head(kh, blk, g):
        s, v3 = env[kh, blk]
        mask_prev, mask_next = env["mask", blk]
        rows = slice(blk * BLOCK, (blk + 1) * BLOCK)
        hq = kh * GQ_B + g
        sink = sink_ref[hq] * LOG2E
        sg = s[g * BLOCK:(g + 1) * BLOCK]
        sg = jnp.concatenate([sg[:, :BLOCK] + mask_prev, sg[:, BLOCK:2 * BLOCK],
                              sg[:, 2 * BLOCK:] + mask_next], axis=1)
        m = jnp.maximum(jnp.max(sg, axis=-1, keepdims=True), sink)
        p = jnp.exp2(sg - m)
        den = jnp.sum(p, axis=-1, keepdims=True) + jnp.exp2(sink - m)
        o = jnp.dot(p.astype(BF16), v3, preferred_element_type=F32) / den
        hc = slice(hq * HD_B, (hq + 1) * HD_B)
        yb_ref[rows, hc] = (o * _silu_of_half(zb_ref[rows, hc].astype(F32))).astype(yb_ref.dtype)

    items = []
    for kh in range(HKV_B):
        for blk in range(MM_TM // BLOCK):
            items.append(functools.partial(logits, kh, blk))
            items += [functools.partial(head, kh, blk, g) for g in range(GQ_B)]
    return items


def _mix_c_items(of_ref, ob_ref, zc_ref, cg_ref, yc_ref):
    def item(h):
        hc = slice(h * HDV_C, (h + 1) * HDV_C)
        oh = of_ref[:, hc].astype(F32) + ob_ref[:, hc].astype(F32)
        ms = jnp.mean(oh * oh, axis=-1, keepdims=True)
        yc_ref[:, hc] = (oh * lax.rsqrt(ms + EPS) * cg_ref[...]
                         * _silu_of_half(zc_ref[:, hc].astype(F32))).astype(yc_ref.dtype)

    return [functools.partial(item, h) for h in range(H_C)]


def _merge_items(ya_ref, yb_ref, yc_ref, ga_ref, gb_ref, gc_ref, x_ref,
                 wpa_ref, wpb_ref, wpc_ref, wout_ref, ng_ref, x_out_ref, h_ref,
                 mg_s, xn_s, ss_s):
    n_chunks = D_MODEL // MERGE_TN

    branches = ((ya_ref, wpa_ref, ga_ref), (yb_ref, wpb_ref, gb_ref), (yc_ref, wpc_ref, gc_ref))
    env = {}

    def branch(c, k):
        cols = slice(c * MERGE_TN, (c + 1) * MERGE_TN)
        y_ref, w_ref, g_ref = branches[k]
        part = ((1.0 + jnp.tanh(g_ref[:, cols].astype(F32)))
                * jnp.dot(y_ref[...], w_ref[:, cols], preferred_element_type=F32))
        acc = part if k == 0 else env[c] + part
        if k == len(branches) - 1:
            mg_s[:, cols] = (0.5 * acc).astype(mg_s.dtype)
        else:
            env[c] = acc

    def out(c):
        cols = slice(c * MERGE_TN, (c + 1) * MERGE_TN)
        xn = x_ref[:, cols] + jnp.dot(mg_s[...], wout_ref[:, cols], preferred_element_type=F32)
        xn_s[:, cols] = xn
        if x_out_ref is not None:
            x_out_ref[:, cols] = xn
        part = jnp.sum(xn * xn, axis=-1, keepdims=True)
        ss_s[...] = part if c == 0 else ss_s[...] + part

    def norm():
        hn = xn_s[...] * lax.rsqrt(ss_s[...] * (1.0 / D_MODEL) + EPS) * ng_ref[...]
        h_ref[...] = hn.astype(h_ref.dtype)

    return ([functools.partial(branch, c, k) for c in range(n_chunks) for k in range(len(branches))]
            + [functools.partial(out, c) for c in range(n_chunks)] + [norm])


def _mixmerge_kernel(sink_ref, m_ref, kvp_ref, kvc_ref, kvn_ref,
                     csp_ref, csc_ref, csn_ref, lng_ref, lnb_ref, ws_ref, bias_ref,
                     of_ref, ob_ref, g_ref, x_ref,
                     wpa_ref, wpb_ref, wpc_ref, wout_ref, cg_ref, ng_ref,
                     *rest, n_tiles, tiles_per_seq, emit_x):
    if emit_x:
        x_out_ref, h_ref, ya_s, yb_s, yc_s, mg_s, xn_s, ss_s = rest
    else:
        x_out_ref = None
        h_ref, ya_s, yb_s, yc_s, mg_s, xn_s, ss_s = rest
    a3_ref = m_ref.at[:, 0:3 * D_A]
    q_ref = m_ref.at[:, 3 * D_A:3 * D_A + D_B]
    zb_ref = m_ref.at[:, 3 * D_A + D_B:3 * D_A + 2 * D_B]
    zc_ref = m_ref.at[:, 3 * D_A + 2 * D_B:MIX_WIDTH]
    ga_ref, gb_ref, gc_ref = [g_ref.at[:, k * D_MODEL:(k + 1) * D_MODEL] for k in range(3)]
    r = pl.program_id(0)
    slot = lax.rem(r, 2)

    @pl.when(r == 0)
    def _():
        ya_s[1] = jnp.zeros(ya_s.shape[1:], ya_s.dtype)
        yb_s[1] = jnp.zeros(yb_s.shape[1:], yb_s.dtype)
        yc_s[1] = jnp.zeros(yc_s.shape[1:], yc_s.dtype)

    pos = lax.rem(jnp.minimum(r, n_tiles - 1), tiles_per_seq)
    first = (pos == 0).astype(jnp.int32)
    final = (pos == tiles_per_seq - 1).astype(jnp.int32)
    mix_a = _mix_a_items(a3_ref, lng_ref, lnb_ref, ws_ref, bias_ref, ya_s.at[slot])
    mix_b = _mix_b_items(sink_ref, q_ref, zb_ref, kvp_ref, kvc_ref, kvn_ref, csp_ref, csc_ref,
                         csn_ref, yb_s.at[slot], first, final)
    mix_c = _mix_c_items(of_ref, ob_ref, zc_ref, cg_ref, yc_s.at[slot])
    merge = _merge_items(ya_s.at[1 - slot], yb_s.at[1 - slot], yc_s.at[1 - slot], ga_ref, gb_ref,
                         gc_ref, x_ref, wpa_ref, wpb_ref, wpc_ref, wout_ref, ng_ref, x_out_ref,
                         h_ref, mg_s, xn_s, ss_s)
    mix = mix_c[:2] + mix_b[:10] + mix_a[:5] + mix_c[2:] + mix_b[10:] + mix_a[5:]
    slots = len(merge) - 1
    done = 0
    for n, merge_item in enumerate(merge):
        merge_item()
        upto = len(mix) if n >= slots - 1 else (len(mix) * (n + 1)) // slots
        for mix_item in mix[done:upto]:
            mix_item()
        done = upto


def _mixmerge(proj, o_f, o_b, x, cs, params, layer, next_g, batch, seq, last):
    t = batch * seq
    n_tiles = t // MM_TM
    tps = seq // MM_TM
    nb2 = MM_TM // BLOCK

    def mix_tile(r):
        return jnp.minimum(r, n_tiles - 1)

    def merge_tile(r):
        return jnp.maximum(r - 1, 0)

    def kv_prev(r):
        m = mix_tile(r)
        return jnp.where(lax.rem(m, tps) == 0, nb2 * m, nb2 * m - 1)

    def kv_next(r):
        m = mix_tile(r)
        return jnp.where(lax.rem(m, tps) == tps - 1, nb2 * m + nb2 - 1, nb2 * m + nb2)

    def cs_prev(r):
        i = lax.rem(mix_tile(r), tps)
        return jnp.maximum(nb2 * i - 1, 0)

    def cs_next(r):
        i = lax.rem(mix_tile(r), tps)
        return jnp.minimum(nb2 * i + nb2, nb2 * tps - 1)

    mix = lambda width, blk: pl.BlockSpec((MM_TM, width), lambda r: (mix_tile(r), blk))
    mrg = lambda width, blk: pl.BlockSpec((MM_TM, width), lambda r: (merge_tile(r), blk))
    per_layer = lambda shape: pl.BlockSpec((None,) + shape, lambda r: (layer,) + (0,) * len(shape))
    weight = lambda shape: pl.BlockSpec((None,) + shape, lambda r: (layer,) + (0,) * len(shape),
                                        pipeline_mode=pl.Buffered(1))
    kvb = COL_KVB // (2 * DKV_B)
    in_specs = [
        pl.BlockSpec(memory_space=pltpu.SMEM),
        mix(MIX_WIDTH, COL_MIX // MIX_WIDTH),
        pl.BlockSpec((BLOCK, 2 * DKV_B), lambda r: (kv_prev(r), kvb)),
        mix(2 * DKV_B, kvb),
        pl.BlockSpec((BLOCK, 2 * DKV_B), lambda r: (kv_next(r), kvb)),
        pl.BlockSpec((BLOCK, 2 * HD_B), lambda r: (cs_prev(r), 0)),
        pl.BlockSpec((MM_TM, 2 * HD_B), lambda r: (lax.rem(mix_tile(r), tps), 0)),
        pl.BlockSpec((BLOCK, 2 * HD_B), lambda r: (cs_next(r), 0)),
        per_layer((1, D_A)), per_layer((1, D_A)), per_layer((G_A, BLOCK, BLOCK)),
        per_layer((BLOCK, D_A)),
        mix(DV_C, 0), mix(DV_C, 0),
        mrg(3 * D_MODEL, COL_GATE // (3 * D_MODEL)),
        mrg(D_MODEL, 0),
        weight((D_A, D_MODEL)), weight((D_B, D_MODEL)), weight((DV_C, D_MODEL)),
        weight((D_MODEL, D_MODEL)), per_layer((1, HDV_C)),
        pl.BlockSpec((1, D_MODEL), lambda r: (0, 0)),
    ]
    h_dtype = F32 if last else BF16
    out_specs = [mrg(D_MODEL, 0)]
    out_shape = [jax.ShapeDtypeStruct((t, D_MODEL), h_dtype)]
    if not last:
        out_specs = [mrg(D_MODEL, 0)] + out_specs
        out_shape = [jax.ShapeDtypeStruct((t, D_MODEL), F32)] + out_shape
    outs = pl.pallas_call(
        functools.partial(_mixmerge_kernel, n_tiles=n_tiles, tiles_per_seq=tps, emit_x=not last),
        grid=(n_tiles + 1,),
        in_specs=in_specs,
        out_specs=out_specs,
        out_shape=out_shape,
        scratch_shapes=[pltpu.VMEM((2, MM_TM, D_A), BF16), pltpu.VMEM((2, MM_TM, D_B), BF16),
                        pltpu.VMEM((2, MM_TM, DV_C), BF16), pltpu.VMEM((MM_TM, D_MODEL), BF16),
                        pltpu.VMEM((MM_TM, D_MODEL), F32), pltpu.VMEM((MM_TM, 1), F32)],
        compiler_params=pltpu.CompilerParams(dimension_semantics=("arbitrary",),
                                             vmem_limit_bytes=VMEM_LIMIT),
        name="mixmerge",
    )(params["b_sink"][layer], proj, proj, proj, proj, cs, cs, cs,
      params["a_ln_g"], params["a_ln_b"], params["a_ws"], params["a_bias"],
      o_f, o_b, proj, x,
      params["w_pa"], params["w_pb"], params["w_pc"], params["w_out"], params["c_norm_g"], next_g)
    return (None, outs[0]) if last else (outs[0], outs[1])


def _rope_table(seq):
    half = HD_B // 2
    inv = ROPE_THETA ** (-jnp.arange(half, dtype=F32) * 2.0 / HD_B)
    ang = jnp.arange(seq, dtype=F32)[:, None] * inv[None, :]
    cos, sin = jnp.cos(ang), jnp.sin(ang)
    return jnp.concatenate([cos, cos, -sin, sin], axis=-1)


def _pad_gate_w(w, row0):
    z = jnp.zeros((w.shape[0], LANES, DK_C), w.dtype)
    return z.at[:, row0:row0 + GATE_RANK].set(w).astype(BF16)


def _prepare(norm_g, w_in, a_ln_g, a_ln_b, a_ws, a_bs, b_sink, c_wf, c_bf, c_wb, c_bb, c_norm_g,
             w_pa, w_pb, w_pc, w_out, final_g):
    return dict(
        norm_g=norm_g, final_g=final_g, w_in=_prep_w_in(w_in),
        a_ln_g=a_ln_g[:, None], a_ln_b=a_ln_b[:, None], a_ws=a_ws.astype(BF16),
        a_bias=jnp.repeat(jnp.swapaxes(a_bs, 1, 2), DG_A, axis=2),
        b_sink=b_sink,
        c_wf=_pad_gate_w(c_wf, 0), c_bf=c_bf, c_wb=_pad_gate_w(c_wb, GATE_RANK), c_bb=c_bb,
        c_norm_g=c_norm_g[:, None], w_pa=w_pa.astype(BF16), w_pb=w_pb.astype(BF16),
        w_pc=w_pc.astype(BF16), w_out=w_out.astype(BF16),
    )


def _trunk(x, params, batch, seq):
    t = batch * seq
    x = x.reshape(t, D_MODEL)
    cs = _rope_table(seq)
    h = _rmsnorm(x, params["norm_g"][0][None])
    for l in range(DEPTH):
        last = l == DEPTH - 1
        proj = _proj(h, params["w_in"], l)
        o_f, o_b = _gla(proj, params["c_wf"][l], params["c_bf"][l][None], params["c_wb"][l],
                        params["c_bb"][l][None], batch, seq)
        next_g = params["final_g"] if last else params["norm_g"][l + 1]
        x, h = _mixmerge(proj, o_f, o_b, x, cs, params, l, next_g[None], batch, seq, last)
    return h.reshape(batch, seq, D_MODEL)


def kernel(x_prompt, x_sample, norm_g, w_in, a_ln_g, a_ln_b, a_ws, a_bs, b_sink, c_wf, c_bf,
           c_wb, c_bb, c_norm_g, w_pa, w_pb, w_pc, w_out, final_g):
    params = _prepare(norm_g, w_in, a_ln_g, a_ln_b, a_ws, a_bs, b_sink, c_wf, c_bf, c_wb, c_bb,
                      c_norm_g, w_pa, w_pb, w_pc, w_out, final_g)
    y_prompt = _trunk(x_prompt, params, x_prompt.shape[0], x_prompt.shape[1])
    y_sample = _trunk(x_sample, params, x_sample.shape[0], x_sample.shape[1])
    return (y_prompt, y_sample)
```

```python
import functools

import numpy as np
import jax
import jax.numpy as jnp
from jax import lax
from jax.experimental import pallas as pl
from jax.experimental.pallas import tpu as pltpu

F32 = jnp.float32
BF16 = jnp.bfloat16

D_MODEL = 2048
DEPTH = 4
EPS = 1e-6
BLOCK = 128
D_A = 1024
G_A = 4
DG_A = D_A // G_A
HQ_B = 8
HKV_B = 2
HD_B = 128
D_B = HQ_B * HD_B
DKV_B = HKV_B * HD_B
GQ_B = HQ_B // HKV_B
ROPE_THETA = 10000.0
H_C = 4
DK_C = 512
DV_C = 1024
HDK_C = DK_C // H_C
HDV_C = DV_C // H_C
GATE_RANK = 16
GATE_TEMP = 16.0
CHUNK_C = 64
IN_SPLITS = (D_A, D_A, D_A, D_B, DKV_B, DKV_B, D_B, DK_C, DK_C, DV_C, DV_C,
             GATE_RANK, GATE_RANK, D_MODEL, D_MODEL, D_MODEL)
N_IN = sum(IN_SPLITS)
SEG_ORDER = (0, 1, 2, 3, 6, 10, 13, 14, 15, 7, 8, 9, 4, 5, 11, 12)
HALVED_SEGS = (2, 6, 10, 13, 14, 15)

COL_MIX = 0
MIX_WIDTH = 3 * D_A + 2 * D_B + DV_C
COL_GATE = COL_MIX + MIX_WIDTH
COL_C = COL_GATE + 3 * D_MODEL
COL_KVB = COL_C + 2 * DK_C + DV_C
COL_LR = COL_KVB + 2 * DKV_B
LANES = 128
N_PROJ = 15360
C_WIDTH = 2 * DK_C + DV_C

PREP_TC = 256
PROJ_TM = 2048
PROJ_TN = 1536
GLA_TB = 1024
GLA_CUM = 256
GLA_LAG = 9
MM_TM = 2 * BLOCK
MERGE_TN = 2048
NEG_BIG = -1e30
LOG2E = 1.4426950408889634
VMEM_LIMIT = 56 * 1024 * 1024


def _silu_of_half(hz):
    return hz + hz * jnp.tanh(hz)


def _prep_segments():
    offs = np.concatenate([[0], np.cumsum(IN_SPLITS)])
    segs, dst = [], 0
    for i in SEG_ORDER:
        segs.append((dst, int(offs[i]), IN_SPLITS[i], 0.5 if i in HALVED_SEGS else 1.0))
        dst += IN_SPLITS[i]
    return segs, dst


def _prep_w_in_kernel(w_ref, o_ref):
    segs, pad0 = _prep_segments()
    for dst, src, width, scale in segs:
        w = w_ref[src:src + width, :]
        if scale != 1.0:
            w = w * scale
        o_ref[dst:dst + width, :] = w.astype(o_ref.dtype)
    o_ref[pad0:, :] = jnp.zeros((N_PROJ - pad0, o_ref.shape[1]), o_ref.dtype)


def _prep_w_in(w_in):
    depth = w_in.shape[0]
    return pl.pallas_call(
        _prep_w_in_kernel,
        grid=(depth, D_MODEL // PREP_TC),
        in_specs=[pl.BlockSpec((None, N_IN, PREP_TC), lambda l, i: (l, 0, i))],
        out_specs=pl.BlockSpec((None, N_PROJ, PREP_TC), lambda l, i: (l, 0, i)),
        out_shape=jax.ShapeDtypeStruct((depth, N_PROJ, D_MODEL), BF16),
        compiler_params=pltpu.CompilerParams(vmem_limit_bytes=VMEM_LIMIT),
        name="prep_w_in",
    )(jnp.swapaxes(w_in, 1, 2))


def _rmsnorm_kernel(x_ref, g_ref, o_ref):
    x = x_ref[...]
    y = x * lax.rsqrt(jnp.mean(x * x, axis=-1, keepdims=True) + EPS) * g_ref[...]
    o_ref[...] = y.astype(o_ref.dtype)


def _rmsnorm(x, g, tm=512):
    t = x.shape[0]
    return pl.pallas_call(
        _rmsnorm_kernel,
        grid=(t // tm,),
        in_specs=[pl.BlockSpec((tm, D_MODEL), lambda i: (i, 0)),
                  pl.BlockSpec((1, D_MODEL), lambda i: (0, 0))],
        out_specs=pl.BlockSpec((tm, D_MODEL), lambda i: (i, 0)),
        out_shape=jax.ShapeDtypeStruct((t, D_MODEL), BF16),
        name="rmsnorm0",
    )(x, g)


def _proj_kernel(h_ref, wt_ref, o_ref):
    o_ref[...] = lax.dot_general(h_ref[...], wt_ref[...], (((1,), (1,)), ((), ())),
                                 preferred_element_type=F32).astype(o_ref.dtype)


def _proj(h, w_all, layer):
    t = h.shape[0]
    return pl.pallas_call(
        _proj_kernel,
        grid=(t // PROJ_TM, N_PROJ // PROJ_TN),
        in_specs=[pl.BlockSpec((PROJ_TM, D_MODEL), lambda i, j: (i, 0)),
                  pl.BlockSpec((None, PROJ_TN, D_MODEL), lambda i, j: (layer, j, 0))],
        out_specs=pl.BlockSpec((PROJ_TM, PROJ_TN), lambda i, j: (i, j)),
        out_shape=jax.ShapeDtypeStruct((t, N_PROJ), BF16),
        compiler_params=pltpu.CompilerParams(vmem_limit_bytes=VMEM_LIMIT),
        name="proj",
    )(h, w_all)


GLA_GROUP = 2 * CHUNK_C


def _cum_matrix(reverse):
    r, c = np.indices((GLA_CUM, GLA_CUM))
    tri = (c >= r) if reverse else (c <= r)
    return jnp.asarray(((r // CHUNK_C == c // CHUNK_C) & tri).astype(np.float32), dtype=BF16)


def _gla_items(c_ref, lr_ref, w_ref, b_ref, cum_ref, o_ref, s_ref, *, reverse):
    nt = (((1,), (1,)), ((), ()))
    tn = (((0,), (0,)), ((), ()))
    last = 0 if reverse else CHUNK_C - 1
    n_groups = GLA_TB // GLA_GROUP
    order = list(range(n_groups - 1, -1, -1) if reverse else range(n_groups))
    env = {}

    def gates():
        x = jnp.dot(lr_ref[...], w_ref[...], preferred_element_type=F32) + b_ref[...]
        g = (jnp.minimum(x, 0.0) * (LOG2E / GATE_TEMP)
             - jnp.log2(1.0 + jnp.exp2(jnp.abs(x) * -LOG2E)) * (1.0 / GATE_TEMP))
        cum = cum_ref[...]
        g_hi = g.astype(BF16)
        g_lo = (g - g_hi.astype(F32)).astype(BF16)
        env["b"] = jnp.concatenate(
            [jnp.dot(cum, g_hi[i:i + GLA_CUM], preferred_element_type=F32)
             + jnp.dot(cum, g_lo[i:i + GLA_CUM], preferred_element_type=F32)
             for i in range(0, GLA_TB, GLA_CUM)], axis=0)
        env["s"] = [s_ref[h] for h in range(H_C)]

    def prep(gi):
        rows0 = slice(gi * GLA_GROUP, gi * GLA_GROUP + CHUNK_C)
        rows1 = slice(gi * GLA_GROUP + CHUNK_C, (gi + 1) * GLA_GROUP)
        b0, b1 = env["b"][rows0], env["b"][rows1]
        bl0, bl1 = b0[last:last + 1], b1[last:last + 1]
        dec0, dec1 = jnp.exp2(bl0), jnp.exp2(bl1)
        q0 = c_ref[rows0, 0:DK_C].astype(F32) * (HDK_C ** -0.5)
        q1 = c_ref[rows1, 0:DK_C].astype(F32) * (HDK_C ** -0.5)
        k0 = c_ref[rows0, DK_C:2 * DK_C].astype(F32)
        k1 = c_ref[rows1, DK_C:2 * DK_C].astype(F32)
        qe0, qe1 = q0 * jnp.exp2(b0), q1 * jnp.exp2(b1)
        ke0, ke1 = (k0 * jnp.exp2(-b0)).astype(BF16), (k1 * jnp.exp2(-b1)).astype(BF16)
        kd0, kd1 = k0 * jnp.exp2(bl0 - b0), k1 * jnp.exp2(bl1 - b1)
        if reverse:
            q_inter = jnp.concatenate([qe0 * dec1, qe1], axis=0).astype(BF16)
            k_state = jnp.concatenate([kd0, kd1 * dec0], axis=0).astype(BF16)
            keys0 = jnp.concatenate([ke0, kd1.astype(BF16)], axis=0)
            keys1 = jnp.concatenate([ke0, ke1], axis=0)
        else:
            q_inter = jnp.concatenate([qe0, qe1 * dec0], axis=0).astype(BF16)
            k_state = jnp.concatenate([kd0 * dec1, kd1], axis=0).astype(BF16)
            keys0 = jnp.concatenate([ke0, ke1], axis=0)
            keys1 = jnp.concatenate([kd0.astype(BF16), ke1], axis=0)
        env["prep", gi] = (qe0.astype(BF16), qe1.astype(BF16), keys0, keys1, q_inter, k_state,
                           dec0 * dec1)

    def intra(gi):
        qe0, qe1, keys0, keys1, q_inter, k_state, dec = env["prep", gi]
        rows = slice(gi * GLA_GROUP, (gi + 1) * GLA_GROUP)
        r2 = lax.broadcasted_iota(jnp.int32, (GLA_GROUP, GLA_GROUP), 0)
        c2 = lax.broadcasted_iota(jnp.int32, (GLA_GROUP, GLA_GROUP), 1)
        keep = (c2 > r2) if reverse else (c2 <= r2)
        per_head = []
        for h in range(H_C):
            kcols = slice(h * HDK_C, (h + 1) * HDK_C)
            v = c_ref[rows, 2 * DK_C + h * HDV_C:2 * DK_C + (h + 1) * HDV_C]
            a = jnp.concatenate(
                [lax.dot_general(qe0[:, kcols], keys0[:, kcols], nt, preferred_element_type=F32),
                 lax.dot_general(qe1[:, kcols], keys1[:, kcols], nt, preferred_element_type=F32)],
                axis=0)
            a = jnp.where(keep, a, 0.0).astype(BF16)
            ds = lax.dot_general(k_state[:, kcols], v, tn, preferred_element_type=F32)
            dec_col = jnp.transpose(jnp.broadcast_to(dec[:, kcols], (8, HDK_C)))[:, 0:1]
            lhs = jnp.concatenate([a, q_inter[:, kcols]], axis=1)
            per_head.append((lhs, ds, dec_col))
        env["intra", gi] = per_head

    def scan(gi, final):
        rows = slice(gi * GLA_GROUP, (gi + 1) * GLA_GROUP)
        states = env["s"]
        for h, (lhs, ds, dec_col) in enumerate(env["intra", gi]):
            v = c_ref[rows, 2 * DK_C + h * HDV_C:2 * DK_C + (h + 1) * HDV_C]
            rhs = jnp.concatenate([v, states[h].astype(BF16)], axis=0)
            o = jnp.dot(lhs, rhs, preferred_element_type=F32)
            o_ref[rows, h * HDV_C:(h + 1) * HDV_C] = o.astype(o_ref.dtype)
            states[h] = dec_col * states[h] + ds
            if final:
                s_ref[h] = states[h]

    items = [gates]
    for gi in order:
        items += [functools.partial(prep, gi), functools.partial(intra, gi)]
    for n, gi in enumerate(order):
        items.append(functools.partial(scan, gi, n == len(order) - 1))
    return items


def _gla_kernel(cf_ref, lrf_ref, cb_ref, lrb_ref, wf_ref, bf_ref, wb_ref, bb_ref, cumf_ref,
                cumb_ref, of_ref, ob_ref, sf_ref, sb_ref):
    @pl.when(pl.program_id(1) == 0)
    def _():
        sf_ref[...] = jnp.zeros_like(sf_ref)
        sb_ref[...] = jnp.zeros_like(sb_ref)

    fwd = _gla_items(cf_ref, lrf_ref, wf_ref, bf_ref, cumf_ref, of_ref, sf_ref, reverse=False)
    bwd = _gla_items(cb_ref, lrb_ref, wb_ref, bb_ref, cumb_ref, ob_ref, sb_ref, reverse=True)
    lag = GLA_LAG
    for n in range(len(fwd) + lag):
        if n < len(fwd):
            fwd[n]()
        if 0 <= n - lag < len(bwd):
            bwd[n - lag]()


def _gla(proj, wf_pad, bf, wb_pad, bb, batch, seq):
    t = batch * seq
    nblk = seq // GLA_TB

    def fwd(b, j):
        return b * nblk + j

    def bwd(b, j):
        return b * nblk + (nblk - 1 - j)

    cblk = COL_C // C_WIDTH
    lrblk = COL_LR // LANES
    in_specs = [
        pl.BlockSpec((GLA_TB, C_WIDTH), lambda b, j: (fwd(b, j), cblk)),
        pl.BlockSpec((GLA_TB, LANES), lambda b, j: (fwd(b, j), lrblk)),
        pl.BlockSpec((GLA_TB, C_WIDTH), lambda b, j: (bwd(b, j), cblk)),
        pl.BlockSpec((GLA_TB, LANES), lambda b, j: (bwd(b, j), lrblk)),
        pl.BlockSpec((LANES, DK_C), lambda b, j: (0, 0)),
        pl.BlockSpec((1, DK_C), lambda b, j: (0, 0)),
        pl.BlockSpec((LANES, DK_C), lambda b, j: (0, 0)),
        pl.BlockSpec((1, DK_C), lambda b, j: (0, 0)),
        pl.BlockSpec((GLA_CUM, GLA_CUM), lambda b, j: (0, 0)),
        pl.BlockSpec((GLA_CUM, GLA_CUM), lambda b, j: (0, 0)),
    ]
    out_specs = [pl.BlockSpec((GLA_TB, DV_C), lambda b, j: (fwd(b, j), 0)),
                 pl.BlockSpec((GLA_TB, DV_C), lambda b, j: (bwd(b, j), 0))]
    return pl.pallas_call(
        _gla_kernel,
        grid=(batch, nblk),
        in_specs=in_specs,
        out_specs=out_specs,
        out_shape=[jax.ShapeDtypeStruct((t, DV_C), BF16), jax.ShapeDtypeStruct((t, DV_C), BF16)],
        scratch_shapes=[pltpu.VMEM((H_C, HDK_C, HDV_C), F32), pltpu.VMEM((H_C, HDK_C, HDV_C), F32)],
        compiler_params=pltpu.CompilerParams(dimension_semantics=("arbitrary", "arbitrary"),
                                             vmem_limit_bytes=VMEM_LIMIT),
        name="gla",
    )(proj, proj, proj, proj, wf_pad, bf, wb_pad, bb, _cum_matrix(False), _cum_matrix(True))


def _mix_a_items(a3_ref, lng_ref, lnb_ref, ws_ref, bias_ref, ya_ref):
    env = {}

    def norm(blk):
        rows = slice(blk * BLOCK, (blk + 1) * BLOCK)
        va = a3_ref[rows, D_A:2 * D_A].astype(F32)
        mu = jnp.mean(va, axis=-1, keepdims=True)
        dv = va - mu
        var = jnp.mean(dv * dv, axis=-1, keepdims=True)
        env[blk] = (dv * lax.rsqrt(var + EPS) * lng_ref[...] + lnb_ref[...]).astype(BF16)

    def group(blk, g):
        rows = slice(blk * BLOCK, (blk + 1) * BLOCK)
        cols = slice(g * DG_A, (g + 1) * DG_A)
        f = jnp.dot(ws_ref[g], env[blk][:, cols], preferred_element_type=F32) + bias_ref[:, cols]
        ua = a3_ref[rows, g * DG_A:(g + 1) * DG_A].astype(F32)
        za = a3_ref[rows, 2 * D_A + g * DG_A:2 * D_A + (g + 1) * DG_A].astype(F32)
        ya_ref[rows, cols] = (ua * f * _silu_of_half(za)).astype(ya_ref.dtype)

    items = []
    for blk in range(MM_TM // BLOCK):
        items.append(functools.partial(norm, blk))
        items += [functools.partial(group, blk, g) for g in range(G_A)]
    return items


def _mix_b_items(sink_ref, q_ref, zb_ref, kvp_ref, kvc_ref, kvn_ref, csp_ref, csc_ref, csn_ref,
                 yb_ref, first, final):
    def rope(x, cs):
        return x * cs[:, :HD_B] + pltpu.roll(x, HD_B // 2, 1) * cs[:, HD_B:]

    q_scale = (HD_B ** -0.5) * LOG2E
    roped = {}
    env = {}

    def logits(kh, blk):
        kcol = slice(kh * HD_B, (kh + 1) * HD_B)
        vcol = slice(DKV_B + kh * HD_B, DKV_B + (kh + 1) * HD_B)
        if kh not in roped:
            cs_k = jnp.concatenate([csp_ref[...], csc_ref[...], csn_ref[...]], axis=0)
            k_raw = jnp.concatenate([kvp_ref[:, kcol], kvc_ref[:, kcol], kvn_ref[:, kcol]], axis=0)
            k4 = rope(k_raw.astype(F32), cs_k).astype(BF16)
            v4 = jnp.concatenate([kvp_ref[:, vcol], kvc_ref[:, vcol], kvn_ref[:, vcol]], axis=0)
            roped[kh] = (k4, v4)
        k4, v4 = roped[kh]
        rows = slice(blk * BLOCK, (blk + 1) * BLOCK)
        if ("mask", blk) not in env:
            il = lax.broadcasted_iota(jnp.int32, (BLOCK, BLOCK), 0)
            jl = lax.broadcasted_iota(jnp.int32, (BLOCK, BLOCK), 1)
            has_prev = 1 if blk > 0 else 1 - first
            has_next = 1 if blk < MM_TM // BLOCK - 1 else 1 - final
            env["mask", blk] = (jnp.where(jl >= il + BLOCK * (1 - has_prev), 0.0, NEG_BIG),
                                jnp.where(jl <= il - BLOCK * (1 - has_next), 0.0, NEG_BIG))
        k3 = k4[blk * BLOCK:(blk + 3) * BLOCK]
        v3 = v4[blk * BLOCK:(blk + 3) * BLOCK]
        cs_q = csc_ref[rows, :]
        qs = jnp.concatenate(
            [rope(q_ref[rows, (kh * GQ_B + g) * HD_B:(kh * GQ_B + g + 1) * HD_B].astype(F32),
                  cs_q) * q_scale for g in range(GQ_B)], axis=0).astype(BF16)
        s = lax.dot_general(qs, k3, (((1,), (1,)), ((), ())), preferred_element_type=F32)
        env[kh, blk] = (s, v3)

    def head(kh, blk, g):
        s, v3 = env[kh, blk]
        mask_prev, mask_next = env["mask", blk]
        rows = slice(blk * BLOCK, (blk + 1) * BLOCK)
        hq = kh * GQ_B + g
        sink = sink_ref[hq] * LOG2E
        sg = s[g * BLOCK:(g + 1) * BLOCK]
        sg = jnp.concatenate([sg[:, :BLOCK] + mask_prev, sg[:, BLOCK:2 * BLOCK],
                              sg[:, 2 * BLOCK:] + mask_next], axis=1)
        m = jnp.maximum(jnp.max(sg, axis=-1, keepdims=True), sink)
        p = jnp.exp2(sg - m)
        den = jnp.sum(p, axis=-1, keepdims=True) + jnp.exp2(sink - m)
        o = jnp.dot(p.astype(BF16), v3, preferred_element_type=F32) / den
        hc = slice(hq * HD_B, (hq + 1) * HD_B)
        yb_ref[rows, hc] = (o * _silu_of_half(zb_ref[rows, hc].astype(F32))).astype(yb_ref.dtype)

    items = []
    for kh in range(HKV_B):
        for blk in range(MM_TM // BLOCK):
            items.append(functools.partial(logits, kh, blk))
            items += [functools.partial(head, kh, blk, g) for g in range(GQ_B)]
    return items


def _mix_c_items(of_ref, ob_ref, zc_ref, cg_ref, yc_ref):
    def item(h):
        hc = slice(h * HDV_C, (h + 1) * HDV_C)
        oh = of_ref[:, hc].astype(F32) + ob_ref[:, hc].astype(F32)
        ms = jnp.mean(oh * oh, axis=-1, keepdims=True)
        yc_ref[:, hc] = (oh * lax.rsqrt(ms + EPS) * cg_ref[...]
                         * _silu_of_half(zc_ref[:, hc].astype(F32))).astype(yc_ref.dtype)

    return [functools.partial(item, h) for h in range(H_C)]


def _merge_items(ya_ref, yb_ref, yc_ref, ga_ref, gb_ref, gc_ref, x_ref,
                 wpa_ref, wpb_ref, wpc_ref, wout_ref, ng_ref, x_out_ref, h_ref,
                 mg_s, xn_s, ss_s):
    n_chunks = D_MODEL // MERGE_TN

    branches = ((ya_ref, wpa_ref, ga_ref), (yb_ref, wpb_ref, gb_ref), (yc_ref, wpc_ref, gc_ref))
    env = {}

    def branch(c, k):
        cols = slice(c * MERGE_TN, (c + 1) * MERGE_TN)
        y_ref, w_ref, g_ref = branches[k]
        part = ((1.0 + jnp.tanh(g_ref[:, cols].astype(F32)))
                * jnp.dot(y_ref[...], w_ref[:, cols], preferred_element_type=F32))
        acc = part if k == 0 else env[c] + part
        if k == len(branches) - 1:
            mg_s[:, cols] = (0.5 * acc).astype(mg_s.dtype)
        else:
            env[c] = acc

    def out(c):
        cols = slice(c * MERGE_TN, (c + 1) * MERGE_TN)
        xn = x_ref[:, cols] + jnp.dot(mg_s[...], wout_ref[:, cols], preferred_element_type=F32)
        xn_s[:, cols] = xn
        if x_out_ref is not None:
            x_out_ref[:, cols] = xn
        part = jnp.sum(xn * xn, axis=-1, keepdims=True)
        ss_s[...] = part if c == 0 else ss_s[...] + part

    def norm():
        hn = xn_s[...] * lax.rsqrt(ss_s[...] * (1.0 / D_MODEL) + EPS) * ng_ref[...]
        h_ref[...] = hn.astype(h_ref.dtype)

    return ([functools.partial(branch, c, k) for c in range(n_chunks) for k in range(len(branches))]
            + [functools.partial(out, c) for c in range(n_chunks)] + [norm])


def _mixmerge_kernel(sink_ref, m_ref, kvp_ref, kvc_ref, kvn_ref,
                     csp_ref, csc_ref, csn_ref, lng_ref, lnb_ref, ws_ref, bias_ref,
                     of_ref, ob_ref, g_ref, x_ref,
                     wpa_ref, wpb_ref, wpc_ref, wout_ref, cg_ref, ng_ref,
                     *rest, n_tiles, tiles_per_seq, emit_x):
    if emit_x:
        x_out_ref, h_ref, ya_s, yb_s, yc_s, mg_s, xn_s, ss_s = rest
    else:
        x_out_ref = None
        h_ref, ya_s, yb_s, yc_s, mg_s, xn_s, ss_s = rest
    a3_ref = m_ref.at[:, 0:3 * D_A]
    q_ref = m_ref.at[:, 3 * D_A:3 * D_A + D_B]
    zb_ref = m_ref.at[:, 3 * D_A + D_B:3 * D_A + 2 * D_B]
    zc_ref = m_ref.at[:, 3 * D_A + 2 * D_B:MIX_WIDTH]
    ga_ref, gb_ref, gc_ref = [g_ref.at[:, k * D_MODEL:(k + 1) * D_MODEL] for k in range(3)]
    r = pl.program_id(0)
    slot = lax.rem(r, 2)

    @pl.when(r == 0)
    def _():
        ya_s[1] = jnp.zeros(ya_s.shape[1:], ya_s.dtype)
        yb_s[1] = jnp.zeros(yb_s.shape[1:], yb_s.dtype)
        yc_s[1] = jnp.zeros(yc_s.shape[1:], yc_s.dtype)

    pos = lax.rem(jnp.minimum(r, n_tiles - 1), tiles_per_seq)
    first = (pos == 0).astype(jnp.int32)
    final = (pos == tiles_per_seq - 1).astype(jnp.int32)
    mix_a = _mix_a_items(a3_ref, lng_ref, lnb_ref, ws_ref, bias_ref, ya_s.at[slot])
    mix_b = _mix_b_items(sink_ref, q_ref, zb_ref, kvp_ref, kvc_ref, kvn_ref, csp_ref, csc_ref,
                         csn_ref, yb_s.at[slot], first, final)
    mix_c = _mix_c_items(of_ref, ob_ref, zc_ref, cg_ref, yc_s.at[slot])
    merge = _merge_items(ya_s.at[1 - slot], yb_s.at[1 - slot], yc_s.at[1 - slot], ga_ref, gb_ref,
                         gc_ref, x_ref, wpa_ref, wpb_ref, wpc_ref, wout_ref, ng_ref, x_out_ref,
                         h_ref, mg_s, xn_s, ss_s)
    mix = mix_c[:2] + mix_b[:10] + mix_a[:5] + mix_c[2:] + mix_b[10:] + mix_a[5:]
    slots = len(merge) - 1
    done = 0
    for n, merge_item in enumerate(merge):
        merge_item()
        upto = len(mix) if n >= slots - 1 else (len(mix) * (n + 1)) // slots
        for mix_item in mix[done:upto]:
            mix_item()
        done = upto


def _mixmerge(proj, o_f, o_b, x, cs, params, layer, next_g, batch, seq, last):
    t = batch * seq
    n_tiles = t // MM_TM
    tps = seq // MM_TM
    nb2 = MM_TM // BLOCK

    def mix_tile(r):
        return jnp.minimum(r, n_tiles - 1)

    def merge_tile(r):
        return jnp.maximum(r - 1, 0)

    def kv_prev(r):
        m = mix_tile(r)
        return jnp.where(lax.rem(m, tps) == 0, nb2 * m, nb2 * m - 1)

    def kv_next(r):
        m = mix_tile(r)
        return jnp.where(lax.rem(m, tps) == tps - 1, nb2 * m + nb2 - 1, nb2 * m + nb2)

    def cs_prev(r):
        i = lax.rem(mix_tile(r), tps)
        return jnp.maximum(nb2 * i - 1, 0)

    def cs_next(r):
        i = lax.rem(mix_tile(r), tps)
        return jnp.minimum(nb2 * i + nb2, nb2 * tps - 1)

    mix = lambda width, blk: pl.BlockSpec((MM_TM, width), lambda r: (mix_tile(r), blk))
    mrg = lambda width, blk: pl.BlockSpec((MM_TM, width), lambda r: (merge_tile(r), blk))
    per_layer = lambda shape: pl.BlockSpec((None,) + shape, lambda r: (layer,) + (0,) * len(shape))
    weight = lambda shape: pl.BlockSpec((None,) + shape, lambda r: (layer,) + (0,) * len(shape),
                                        pipeline_mode=pl.Buffered(1))
    kvb = COL_KVB // (2 * DKV_B)
    in_specs = [
        pl.BlockSpec(memory_space=pltpu.SMEM),
        mix(MIX_WIDTH, COL_MIX // MIX_WIDTH),
        pl.BlockSpec((BLOCK, 2 * DKV_B), lambda r: (kv_prev(r), kvb)),
        mix(2 * DKV_B, kvb),
        pl.BlockSpec((BLOCK, 2 * DKV_B), lambda r: (kv_next(r), kvb)),
        pl.BlockSpec((BLOCK, 2 * HD_B), lambda r: (cs_prev(r), 0)),
        pl.BlockSpec((MM_TM, 2 * HD_B), lambda r: (lax.rem(mix_tile(r), tps), 0)),
        pl.BlockSpec((BLOCK, 2 * HD_B), lambda r: (cs_next(r), 0)),
        per_layer((1, D_A)), per_layer((1, D_A)), per_layer((G_A, BLOCK, BLOCK)),
        per_layer((BLOCK, D_A)),
        mix(DV_C, 0), mix(DV_C, 0),
        mrg(3 * D_MODEL, COL_GATE // (3 * D_MODEL)),
        mrg(D_MODEL, 0),
        weight((D_A, D_MODEL)), weight((D_B, D_MODEL)), weight((DV_C, D_MODEL)),
        weight((D_MODEL, D_MODEL)), per_layer((1, HDV_C)),
        pl.BlockSpec((1, D_MODEL), lambda r: (0, 0)),
    ]
    h_dtype = F32 if last else BF16
    out_specs = [mrg(D_MODEL, 0)]
    out_shape = [jax.ShapeDtypeStruct((t, D_MODEL), h_dtype)]
    if not last:
        out_specs = [mrg(D_MODEL, 0)] + out_specs
        out_shape = [jax.ShapeDtypeStruct((t, D_MODEL), F32)] + out_shape
    outs = pl.pallas_call(
        functools.partial(_mixmerge_kernel, n_tiles=n_tiles, tiles_per_seq=tps, emit_x=not last),
        grid=(n_tiles + 1,),
        in_specs=in_specs,
        out_specs=out_specs,
        out_shape=out_shape,
        scratch_shapes=[pltpu.VMEM((2, MM_TM, D_A), BF16), pltpu.VMEM((2, MM_TM, D_B), BF16),
                        pltpu.VMEM((2, MM_TM, DV_C), BF16), pltpu.VMEM((MM_TM, D_MODEL), BF16),
                        pltpu.VMEM((MM_TM, D_MODEL), F32), pltpu.VMEM((MM_TM, 1), F32)],
        compiler_params=pltpu.CompilerParams(dimension_semantics=("arbitrary",),
                                             vmem_limit_bytes=VMEM_LIMIT),
        name="mixmerge",
    )(params["b_sink"][layer], proj, proj, proj, proj, cs, cs, cs,
      params["a_ln_g"], params["a_ln_b"], params["a_ws"], params["a_bias"],
      o_f, o_b, proj, x,
      params["w_pa"], params["w_pb"], params["w_pc"], params["w_out"], params["c_norm_g"], next_g)
    return (None, outs[0]) if last else (outs[0], outs[1])


def _rope_table(seq):
    half = HD_B // 2
    inv = ROPE_THETA ** (-jnp.arange(half, dtype=F32) * 2.0 / HD_B)
    ang = jnp.arange(seq, dtype=F32)[:, None] * inv[None, :]
    cos, sin = jnp.cos(ang), jnp.sin(ang)
    return jnp.concatenate([cos, cos, -sin, sin], axis=-1)


def _pad_gate_w(w, row0):
    z = jnp.zeros((w.shape[0], LANES, DK_C), w.dtype)
    return z.at[:, row0:row0 + GATE_RANK].set(w).astype(BF16)


def _prepare(norm_g, w_in, a_ln_g, a_ln_b, a_ws, a_bs, b_sink, c_wf, c_bf, c_wb, c_bb, c_norm_g,
             w_pa, w_pb, w_pc, w_out, final_g):
    return dict(
        norm_g=norm_g, final_g=final_g, w_in=_prep_w_in(w_in),
        a_ln_g=a_ln_g[:, None], a_ln_b=a_ln_b[:, None], a_ws=a_ws.astype(BF16),
        a_bias=jnp.repeat(jnp.swapaxes(a_bs, 1, 2), DG_A, axis=2),
        b_sink=b_sink,
        c_wf=_pad_gate_w(c_wf, 0), c_bf=c_bf, c_wb=_pad_gate_w(c_wb, GATE_RANK), c_bb=c_bb,
        c_norm_g=c_norm_g[:, None], w_pa=w_pa.astype(BF16), w_pb=w_pb.astype(BF16),
        w_pc=w_pc.astype(BF16), w_out=w_out.astype(BF16),
    )


def _trunk(x, params, batch, seq):
    t = batch * seq
    x = x.reshape(t, D_MODEL)
    cs = _rope_table(seq)
    h = _rmsnorm(x, params["norm_g"][0][None])
    for l in range(DEPTH):
        last = l == DEPTH - 1
        proj = _proj(h, params["w_in"], l)
        o_f, o_b = _gla(proj, params["c_wf"][l], params["c_bf"][l][None], params["c_wb"][l],
                        params["c_bb"][l][None], batch, seq)
        next_g = params["final_g"] if last else params["norm_g"][l + 1]
        x, h = _mixmerge(proj, o_f, o_b, x, cs, params, l, next_g[None], batch, seq, last)
    return h.reshape(batch, seq, D_MODEL)


def kernel(x_prompt, x_sample, norm_g, w_in, a_ln_g, a_ln_b, a_ws, a_bs, b_sink, c_wf, c_bf,
           c_wb, c_bb, c_norm_g, w_pa, w_pb, w_pc, w_out, final_g):
    params = _prepare(norm_g, w_in, a_ln_g, a_ln_b, a_ws, a_bs, b_sink, c_wf, c_bf, c_wb, c_bb,
                      c_norm_g, w_pa, w_pb, w_pc, w_out, final_g)
    y_prompt = _trunk(x_prompt, params, x_prompt.shape[0], x_prompt.shape[1])
    y_sample = _trunk(x_sample, params, x_sample.shape[0], x_sample.shape[1])
    return (y_prompt, y_sample)
```

```python
import functools

import numpy as np
import jax
import jax.numpy as jnp
from jax import lax
from jax.experimental import pallas as pl
from jax.experimental.pallas import tpu as pltpu

F32 = jnp.float32
BF16 = jnp.bfloat16

D_MODEL = 2048
DEPTH = 4
EPS = 1e-6
BLOCK = 128
D_A = 1024
G_A = 4
DG_A = D_A // G_A
HQ_B = 8
HKV_B = 2
HD_B = 128
D_B = HQ_B * HD_B
DKV_B = HKV_B * HD_B
GQ_B = HQ_B // HKV_B
ROPE_THETA = 10000.0
H_C = 4
DK_C = 512
DV_C = 1024
HDK_C = DK_C // H_C
HDV_C = DV_C // H_C
GATE_RANK = 16
GATE_TEMP = 16.0
CHUNK_C = 64
IN_SPLITS = (D_A, D_A, D_A, D_B, DKV_B, DKV_B, D_B, DK_C, DK_C, DV_C, DV_C,
             GATE_RANK, GATE_RANK, D_MODEL, D_MODEL, D_MODEL)
N_IN = sum(IN_SPLITS)
SEG_ORDER = (0, 1, 2, 3, 6, 10, 13, 14, 15, 7, 8, 9, 4, 5, 11, 12)
HALVED_SEGS = (2, 6, 10, 13, 14, 15)

COL_MIX = 0
MIX_WIDTH = 3 * D_A + 2 * D_B + DV_C
COL_GATE = COL_MIX + MIX_WIDTH
COL_C = COL_GATE + 3 * D_MODEL
COL_KVB = COL_C + 2 * DK_C + DV_C
COL_LR = COL_KVB + 2 * DKV_B
LANES = 128
SUBLANES = 8
N_PROJ = 15360
C_WIDTH = 2 * DK_C + DV_C

PREP_TC = 256
PROJ_TM = 2048
PROJ_TN = 1536
GLA_TB = 1024
GLA_CUM = 256
GLA_LAG = 9
MM_TM = 2 * BLOCK
MERGE_TN = 2048
NEG_BIG = -1e30
LOG2E = 1.4426950408889634
V7X_VMEM_BYTES = 64 * 1024 * 1024
VMEM_LIMIT = V7X_VMEM_BYTES * 7 // 8


def _silu_of_half(hz):
    return hz + hz * jnp.tanh(hz)


def _prep_segments():
    offs = np.concatenate([[0], np.cumsum(IN_SPLITS)])
    segs, dst = [], 0
    for i in SEG_ORDER:
        segs.append((dst, int(offs[i]), IN_SPLITS[i], 0.5 if i in HALVED_SEGS else 1.0))
        dst += IN_SPLITS[i]
    return segs, dst


def _prep_w_in_kernel(wt_ref, o_ref):
    segs, pad0 = _prep_segments()
    for dst, src, width, scale in segs:
        if width < LANES:
            continue
        w = wt_ref[src:src + width, :]
        if scale != 1.0:
            w = w * scale
        o_ref[:, dst:dst + width] = w.T.astype(o_ref.dtype)
    (dst_lr, src_lr, _, _), lr_width = segs[-2], 2 * GATE_RANK
    slab = wt_ref[src_lr:src_lr + LANES, :].T
    o_ref[:, dst_lr:dst_lr + lr_width] = slab[:, 0:lr_width].astype(o_ref.dtype)
    o_ref[:, pad0:] = jnp.zeros((o_ref.shape[0], N_PROJ - pad0), o_ref.dtype)


def _prep_w_in(w_in):
    depth = w_in.shape[0]
    return pl.pallas_call(
        _prep_w_in_kernel,
        grid=(depth, D_MODEL // PREP_TC),
        in_specs=[pl.BlockSpec((None, N_IN, PREP_TC), lambda l, i: (l, 0, i))],
        out_specs=pl.BlockSpec((None, PREP_TC, N_PROJ), lambda l, i: (l, i, 0)),
        out_shape=jax.ShapeDtypeStruct((depth, D_MODEL, N_PROJ), BF16),
        compiler_params=pltpu.CompilerParams(vmem_limit_bytes=VMEM_LIMIT),
        name="prep_w_in",
    )(jnp.swapaxes(w_in, 1, 2))


def _rmsnorm_kernel(x_ref, g_ref, o_ref):
    x = x_ref[...]
    y = x * lax.rsqrt(jnp.mean(x * x, axis=-1, keepdims=True) + EPS) * g_ref[...]
    o_ref[...] = y.astype(o_ref.dtype)


def _rmsnorm(x, g, tm=512):
    t = x.shape[0]
    return pl.pallas_call(
        _rmsnorm_kernel,
        grid=(t // tm,),
        in_specs=[pl.BlockSpec((tm, D_MODEL), lambda i: (i, 0)),
                  pl.BlockSpec((1, D_MODEL), lambda i: (0, 0))],
        out_specs=pl.BlockSpec((tm, D_MODEL), lambda i: (i, 0)),
        out_shape=jax.ShapeDtypeStruct((t, D_MODEL), BF16),
        name="rmsnorm0",
    )(x, g)


def _proj_kernel(h_ref, w_ref, o_ref):
    o_ref[...] = jnp.dot(h_ref[...], w_ref[...], preferred_element_type=F32).astype(o_ref.dtype)


def _proj(h, w_all, layer):
    t = h.shape[0]
    return pl.pallas_call(
        _proj_kernel,
        grid=(t // PROJ_TM, N_PROJ // PROJ_TN),
        in_specs=[pl.BlockSpec((PROJ_TM, D_MODEL), lambda i, j: (i, 0)),
                  pl.BlockSpec((None, D_MODEL, PROJ_TN), lambda i, j: (layer, 0, j))],
        out_specs=pl.BlockSpec((PROJ_TM, PROJ_TN), lambda i, j: (i, j)),
        out_shape=jax.ShapeDtypeStruct((t, N_PROJ), BF16),
        compiler_params=pltpu.CompilerParams(vmem_limit_bytes=VMEM_LIMIT),
        name="proj",
    )(h, w_all)


GLA_GROUP = 2 * CHUNK_C


def _cum_matrix(reverse):
    r, c = np.indices((GLA_CUM, GLA_CUM))
    tri = (c >= r) if reverse else (c <= r)
    return jnp.asarray(((r // CHUNK_C == c // CHUNK_C) & tri).astype(np.float32), dtype=BF16)


def _gla_items(c_ref, lr_ref, w_ref, b_ref, cum_ref, o_ref, s_ref, *, reverse):
    nt = (((1,), (1,)), ((), ()))
    tn = (((0,), (0,)), ((), ()))
    last = 0 if reverse else CHUNK_C - 1
    n_groups = GLA_TB // GLA_GROUP
    order = list(range(n_groups - 1, -1, -1) if reverse else range(n_groups))
    env = {}

    def gates():
        x = jnp.dot(lr_ref[...], w_ref[...], preferred_element_type=F32) + b_ref[...]
        g = (jnp.minimum(x, 0.0) * (LOG2E / GATE_TEMP)
             - jnp.log2(1.0 + jnp.exp2(jnp.abs(x) * -LOG2E)) * (1.0 / GATE_TEMP))
        cum = cum_ref[...]
        g_hi = g.astype(BF16)
        g_lo = (g - g_hi.astype(F32)).astype(BF16)
        env["b"] = jnp.concatenate(
            [jnp.dot(cum, g_hi[i:i + GLA_CUM], preferred_element_type=F32)
             + jnp.dot(cum, g_lo[i:i + GLA_CUM], preferred_element_type=F32)
             for i in range(0, GLA_TB, GLA_CUM)], axis=0)
        env["s"] = [s_ref[h] for h in range(H_C)]

    def prep(gi):
        rows0 = slice(gi * GLA_GROUP, gi * GLA_GROUP + CHUNK_C)
        rows1 = slice(gi * GLA_GROUP + CHUNK_C, (gi + 1) * GLA_GROUP)
        b0, b1 = env["b"][rows0], env["b"][rows1]
        bl0, bl1 = b0[last:last + 1], b1[last:last + 1]
        dec0, dec1 = jnp.exp2(bl0), jnp.exp2(bl1)
        q0 = c_ref[rows0, 0:DK_C].astype(F32) * (HDK_C ** -0.5)
        q1 = c_ref[rows1, 0:DK_C].astype(F32) * (HDK_C ** -0.5)
        k0 = c_ref[rows0, DK_C:2 * DK_C].astype(F32)
        k1 = c_ref[rows1, DK_C:2 * DK_C].astype(F32)
        qe0, qe1 = q0 * jnp.exp2(b0), q1 * jnp.exp2(b1)
        ke0, ke1 = (k0 * jnp.exp2(-b0)).astype(BF16), (k1 * jnp.exp2(-b1)).astype(BF16)
        kd0, kd1 = k0 * jnp.exp2(bl0 - b0), k1 * jnp.exp2(bl1 - b1)
        if reverse:
            q_inter = jnp.concatenate([qe0 * dec1, qe1], axis=0).astype(BF16)
            k_state = jnp.concatenate([kd0, kd1 * dec0], axis=0).astype(BF16)
            keys0 = jnp.concatenate([ke0, kd1.astype(BF16)], axis=0)
            keys1 = jnp.concatenate([ke0, ke1], axis=0)
        else:
            q_inter = jnp.concatenate([qe0, qe1 * dec0], axis=0).astype(BF16)
            k_state = jnp.concatenate([kd0 * dec1, kd1], axis=0).astype(BF16)
            keys0 = jnp.concatenate([ke0, ke1], axis=0)
            keys1 = jnp.concatenate([kd0.astype(BF16), ke1], axis=0)
        env["prep", gi] = (qe0.astype(BF16), qe1.astype(BF16), keys0, keys1, q_inter, k_state,
                           dec0 * dec1)

    def intra(gi):
        qe0, qe1, keys0, keys1, q_inter, k_state, dec = env["prep", gi]
        rows = slice(gi * GLA_GROUP, (gi + 1) * GLA_GROUP)
        r2 = lax.broadcasted_iota(jnp.int32, (GLA_GROUP, GLA_GROUP), 0)
        c2 = lax.broadcasted_iota(jnp.int32, (GLA_GROUP, GLA_GROUP), 1)
        keep = (c2 > r2) if reverse else (c2 <= r2)
        per_head = []
        for h in range(H_C):
            kcols = slice(h * HDK_C, (h + 1) * HDK_C)
            v = c_ref[rows, 2 * DK_C + h * HDV_C:2 * DK_C + (h + 1) * HDV_C]
            a = jnp.concatenate(
                [lax.dot_general(qe0[:, kcols], keys0[:, kcols], nt, preferred_element_type=F32),
                 lax.dot_general(qe1[:, kcols], keys1[:, kcols], nt, preferred_element_type=F32)],
                axis=0)
            a = jnp.where(keep, a, 0.0).astype(BF16)
            ds = lax.dot_general(k_state[:, kcols], v, tn, preferred_element_type=F32)
            dec_col = jnp.transpose(jnp.broadcast_to(dec[:, kcols], (SUBLANES, HDK_C)))[:, 0:1]
            lhs = jnp.concatenate([a, q_inter[:, kcols]], axis=1)
            per_head.append((lhs, ds, dec_col))
        env["intra", gi] = per_head

    def scan(gi, final):
        rows = slice(gi * GLA_GROUP, (gi + 1) * GLA_GROUP)
        states = env["s"]
        for h, (lhs, ds, dec_col) in enumerate(env["intra", gi]):
            v = c_ref[rows, 2 * DK_C + h * HDV_C:2 * DK_C + (h + 1) * HDV_C]
            rhs = jnp.concatenate([v, states[h].astype(BF16)], axis=0)
            o = jnp.dot(lhs, rhs, preferred_element_type=F32)
            o_ref[rows, h * HDV_C:(h + 1) * HDV_C] = o.astype(o_ref.dtype)
            states[h] = dec_col * states[h] + ds
            if final:
                s_ref[h] = states[h]

    items = [gates]
    for gi in order:
        items += [functools.partial(prep, gi), functools.partial(intra, gi)]
    for n, gi in enumerate(order):
        items.append(functools.partial(scan, gi, n == len(order) - 1))
    return items


def _gla_kernel(cf_ref, lrf_ref, cb_ref, lrb_ref, wf_ref, bf_ref, wb_ref, bb_ref, cumf_ref,
                cumb_ref, of_ref, ob_ref, sf_ref, sb_ref):
    @pl.when(pl.program_id(1) == 0)
    def _():
        sf_ref[...] = jnp.zeros_like(sf_ref)
        sb_ref[...] = jnp.zeros_like(sb_ref)

    fwd = _gla_items(cf_ref, lrf_ref, wf_ref, bf_ref, cumf_ref, of_ref, sf_ref, reverse=False)
    bwd = _gla_items(cb_ref, lrb_ref, wb_ref, bb_ref, cumb_ref, ob_ref, sb_ref, reverse=True)
    lag = GLA_LAG
    for n in range(len(fwd) + lag):
        if n < len(fwd):
            fwd[n]()
        if 0 <= n - lag < len(bwd):
            bwd[n - lag]()


def _gla(proj, wf_pad, bf, wb_pad, bb, batch, seq):
    t = batch * seq
    nblk = seq // GLA_TB

    def fwd(b, j):
        return b * nblk + j

    def bwd(b, j):
        return b * nblk + (nblk - 1 - j)

    cblk = COL_C // C_WIDTH
    lrblk = COL_LR // LANES
    in_specs = [
        pl.BlockSpec((GLA_TB, C_WIDTH), lambda b, j: (fwd(b, j), cblk)),
        pl.BlockSpec((GLA_TB, LANES), lambda b, j: (fwd(b, j), lrblk)),
        pl.BlockSpec((GLA_TB, C_WIDTH), lambda b, j: (bwd(b, j), cblk)),
        pl.BlockSpec((GLA_TB, LANES), lambda b, j: (bwd(b, j), lrblk)),
        pl.BlockSpec((LANES, DK_C), lambda b, j: (0, 0)),
        pl.BlockSpec((1, DK_C), lambda b, j: (0, 0)),
        pl.BlockSpec((LANES, DK_C), lambda b, j: (0, 0)),
        pl.BlockSpec((1, DK_C), lambda b, j: (0, 0)),
        pl.BlockSpec((GLA_CUM, GLA_CUM), lambda b, j: (0, 0)),
        pl.BlockSpec((GLA_CUM, GLA_CUM), lambda b, j: (0, 0)),
    ]
    out_specs = [pl.BlockSpec((GLA_TB, DV_C), lambda b, j: (fwd(b, j), 0)),
                 pl.BlockSpec((GLA_TB, DV_C), lambda b, j: (bwd(b, j), 0))]
    return pl.pallas_call(
        _gla_kernel,
        grid=(batch, nblk),
        in_specs=in_specs,
        out_specs=out_specs,
        out_shape=[jax.ShapeDtypeStruct((t, DV_C), BF16), jax.ShapeDtypeStruct((t, DV_C), BF16)],
        scratch_shapes=[pltpu.VMEM((H_C, HDK_C, HDV_C), F32), pltpu.VMEM((H_C, HDK_C, HDV_C), F32)],
        compiler_params=pltpu.CompilerParams(dimension_semantics=("arbitrary", "arbitrary"),
                                             vmem_limit_bytes=VMEM_LIMIT),
        name="gla",
    )(proj, proj, proj, proj, wf_pad, bf, wb_pad, bb, _cum_matrix(False), _cum_matrix(True))


def _mix_a_items(a3_ref, lng_ref, lnb_ref, ws_ref, bias_ref, ya_ref):
    env = {}

    def norm(blk):
        rows = slice(blk * BLOCK, (blk + 1) * BLOCK)
        va = a3_ref[rows, D_A:2 * D_A].astype(F32)
        mu = jnp.mean(va, axis=-1, keepdims=True)
        dv = va - mu
        var = jnp.mean(dv * dv, axis=-1, keepdims=True)
        env[blk] = (dv * lax.rsqrt(var + EPS) * lng_ref[...] + lnb_ref[...]).astype(BF16)

    def group(blk, g):
        rows = slice(blk * BLOCK, (blk + 1) * BLOCK)
        cols = slice(g * DG_A, (g + 1) * DG_A)
        f = jnp.dot(ws_ref[g], env[blk][:, cols], preferred_element_type=F32) + bias_ref[:, cols]
        ua = a3_ref[rows, g * DG_A:(g + 1) * DG_A].astype(F32)
        za = a3_ref[rows, 2 * D_A + g * DG_A:2 * D_A + (g + 1) * DG_A].astype(F32)
        ya_ref[rows, cols] = (ua * f * _silu_of_half(za)).astype(ya_ref.dtype)

    items = []
    for blk in range(MM_TM // BLOCK):
        items.append(functools.partial(norm, blk))
        items += [functools.partial(group, blk, g) for g in range(G_A)]
    return items


def _mix_b_items(sink_ref, q_ref, zb_ref, kvp_ref, kvc_ref, kvn_ref, csp_ref, csc_ref, csn_ref,
                 yb_ref, first, final):
    def rope(x, cs):
        return x * cs[:, :HD_B] + pltpu.roll(x, HD_B // 2, 1) * cs[:, HD_B:]

    q_scale = (HD_B ** -0.5) * LOG2E
    roped = {}
    env = {}

    def logits(kh, blk):
        kcol = slice(kh * HD_B, (kh + 1) * HD_B)
        vcol = slice(DKV_B + kh * HD_B, DKV_B + (kh + 1) * HD_B)
        if kh not in roped:
            cs_k = jnp.concatenate([csp_ref[...], csc_ref[...], csn_ref[...]], axis=0)
            k_raw = jnp.concatenate([kvp_ref[:, kcol], kvc_ref[:, kcol], kvn_ref[:, kcol]], axis=0)
            k4 = rope(k_raw.astype(F32), cs_k).astype(BF16)
            v4 = jnp.concatenate([kvp_ref[:, vcol], kvc_ref[:, vcol], kvn_ref[:, vcol]], axis=0)
            roped[kh] = (k4, v4)
        k4, v4 = roped[kh]
        rows = slice(blk * BLOCK, (blk + 1) * BLOCK)
        if ("mask", blk) not in env:
            il = lax.broadcasted_iota(jnp.int32, (BLOCK, BLOCK), 0)
            jl = lax.broadcasted_iota(jnp.int32, (BLOCK, BLOCK), 1)
            has_prev = 1 if blk > 0 else 1 - first
            has_next = 1 if blk < MM_TM // BLOCK - 1 else 1 - final
            env["mask", blk] = (jnp.where(jl >= il + BLOCK * (1 - has_prev), 0.0, NEG_BIG),
                                jnp.where(jl <= il - BLOCK * (1 - has_next), 0.0, NEG_BIG))
        k3 = k4[blk * BLOCK:(blk + 3) * BLOCK]
        v3 = v4[blk * BLOCK:(blk + 3) * BLOCK]
        cs_q = csc_ref[rows, :]
        qs = jnp.concatenate(
            [rope(q_ref[rows, (kh * GQ_B + g) * HD_B:(kh * GQ_B + g + 1) * HD_B].astype(F32),
                  cs_q) * q_scale for g in range(GQ_B)], axis=0).astype(BF16)
        s = lax.dot_general(qs, k3, (((1,), (1,)), ((), ())), preferred_element_type=F32)
        env[kh, blk] = (s, v3)

    def head(kh, blk, g):
        s, v3 = env[kh, blk]
        mask_prev, mask_next = env["mask", blk]
        rows = slice(blk * BLOCK, (blk + 1) * BLOCK)
        hq = kh * GQ_B + g
        sink = sink_ref[hq] * LOG2E
        sg = s[g * BLOCK:(g + 1) * BLOCK]
        sg = jnp.concatenate([sg[:, :BLOCK] + mask_prev, sg[:, BLOCK:2 * BLOCK],
                              sg[:, 2 * BLOCK:] + mask_next], axis=1)
        m = jnp.maximum(jnp.max(sg, axis=-1, keepdims=True), sink)
        p = jnp.exp2(sg - m)
        den = jnp.sum(p, axis=-1, keepdims=True) + jnp.exp2(sink - m)
        o = jnp.dot(p.astype(BF16), v3, preferred_element_type=F32) / den
        hc = slice(hq * HD_B, (hq + 1) * HD_B)
        yb_ref[rows, hc] = (o * _silu_of_half(zb_ref[rows, hc].astype(F32))).astype(yb_ref.dtype)

    items = []
    for kh in range(HKV_B):
        for blk in range(MM_TM // BLOCK):
            items.append(functools.partial(logits, kh, blk))
            items += [functools.partial(head, kh, blk, g) for g in range(GQ_B)]
    return items


def _mix_c_items(of_ref, ob_ref, zc_ref, cg_ref, yc_ref):
    def item(h):
        hc = slice(h * HDV_C, (h + 1) * HDV_C)
        oh = of_ref[:, hc].astype(F32) + ob_ref[:, hc].astype(F32)
        ms = jnp.mean(oh * oh, axis=-1, keepdims=True)
        yc_ref[:, hc] = (oh * lax.rsqrt(ms + EPS) * cg_ref[...]
                         * _silu_of_half(zc_ref[:, hc].astype(F32))).astype(yc_ref.dtype)

    return [functools.partial(item, h) for h in range(H_C)]


def _merge_items(ya_ref, yb_ref, yc_ref, ga_ref, gb_ref, gc_ref, x_ref,
                 wpa_ref, wpb_ref, wpc_ref, wout_ref, ng_ref, x_out_ref, h_ref,
                 mg_s, xn_s, ss_s):
    n_chunks = D_MODEL // MERGE_TN

    branches = ((ya_ref, wpa_ref, ga_ref), (yb_ref, wpb_ref, gb_ref), (yc_ref, wpc_ref, gc_ref))
    env = {}

    def branch(c, k):
        cols = slice(c * MERGE_TN, (c + 1) * MERGE_TN)
        y_ref, w_ref, g_ref = branches[k]
        part = ((1.0 + jnp.tanh(g_ref[:, cols].astype(F32)))
                * jnp.dot(y_ref[...], w_ref[:, cols], preferred_element_type=F32))
        acc = part if k == 0 else env[c] + part
        if k == len(branches) - 1:
            mg_s[:, cols] = (0.5 * acc).astype(mg_s.dtype)
        else:
            env[c] = acc

    def out(c):
        cols = slice(c * MERGE_TN, (c + 1) * MERGE_TN)
        xn = x_ref[:, cols] + jnp.dot(mg_s[...], wout_ref[:, cols], preferred_element_type=F32)
        xn_s[:, cols] = xn
        if x_out_ref is not None:
            x_out_ref[:, cols] = xn
        part = jnp.sum(xn * xn, axis=-1, keepdims=True)
        ss_s[...] = part if c == 0 else ss_s[...] + part

    def norm():
        hn = xn_s[...] * lax.rsqrt(ss_s[...] * (1.0 / D_MODEL) + EPS) * ng_ref[...]
        h_ref[...] = hn.astype(h_ref.dtype)

    return ([functools.partial(branch, c, k) for c in range(n_chunks) for k in range(len(branches))]
            + [functools.partial(out, c) for c in range(n_chunks)] + [norm])


def _mixmerge_kernel(sink_ref, m_ref, kvp_ref, kvc_ref, kvn_ref,
                     csp_ref, csc_ref, csn_ref, lng_ref, lnb_ref, ws_ref, bias_ref,
                     of_ref, ob_ref, g_ref, x_ref,
                     wpa_ref, wpb_ref, wpc_ref, wout_ref, cg_ref, ng_ref,
                     *rest, n_tiles, tiles_per_seq, emit_x):
    if emit_x:
        x_out_ref, h_ref, ya_s, yb_s, yc_s, mg_s, xn_s, ss_s = rest
    else:
        x_out_ref = None
        h_ref, ya_s, yb_s, yc_s, mg_s, xn_s, ss_s = rest
    a3_ref = m_ref.at[:, 0:3 * D_A]
    q_ref = m_ref.at[:, 3 * D_A:3 * D_A + D_B]
    zb_ref = m_ref.at[:, 3 * D_A + D_B:3 * D_A + 2 * D_B]
    zc_ref = m_ref.at[:, 3 * D_A + 2 * D_B:MIX_WIDTH]
    ga_ref, gb_ref, gc_ref = [g_ref.at[:, k * D_MODEL:(k + 1) * D_MODEL] for k in range(3)]
    r = pl.program_id(0)
    slot = lax.rem(r, 2)

    @pl.when(r == 0)
    def _():
        ya_s[1] = jnp.zeros(ya_s.shape[1:], ya_s.dtype)
        yb_s[1] = jnp.zeros(yb_s.shape[1:], yb_s.dtype)
        yc_s[1] = jnp.zeros(yc_s.shape[1:], yc_s.dtype)

    pos = lax.rem(jnp.minimum(r, n_tiles - 1), tiles_per_seq)
    first = (pos == 0).astype(jnp.int32)
    final = (pos == tiles_per_seq - 1).astype(jnp.int32)
    mix_a = _mix_a_items(a3_ref, lng_ref, lnb_ref, ws_ref, bias_ref, ya_s.at[slot])
    mix_b = _mix_b_items(sink_ref, q_ref, zb_ref, kvp_ref, kvc_ref, kvn_ref, csp_ref, csc_ref,
                         csn_ref, yb_s.at[slot], first, final)
    mix_c = _mix_c_items(of_ref, ob_ref, zc_ref, cg_ref, yc_s.at[slot])
    merge = _merge_items(ya_s.at[1 - slot], yb_s.at[1 - slot], yc_s.at[1 - slot], ga_ref, gb_ref,
                         gc_ref, x_ref, wpa_ref, wpb_ref, wpc_ref, wout_ref, ng_ref, x_out_ref,
                         h_ref, mg_s, xn_s, ss_s)
    mix = mix_c[:2] + mix_b[:10] + mix_a[:5] + mix_c[2:] + mix_b[10:] + mix_a[5:]
    slots = len(merge) - 1
    done = 0
    for n, merge_item in enumerate(merge):
        merge_item()
        upto = len(mix) if n >= slots - 1 else (len(mix) * (n + 1)) // slots
        for mix_item in mix[done:upto]:
            mix_item()
        done = upto


def _mixmerge(proj, o_f, o_b, x, cs, params, layer, next_g, batch, seq, last):
    t = batch * seq
    n_tiles = t // MM_TM
    tps = seq // MM_TM
    nb2 = MM_TM // BLOCK

    def mix_tile(r):
        return jnp.minimum(r, n_tiles - 1)

    def merge_tile(r):
        return jnp.maximum(r - 1, 0)

    def kv_prev(r):
        m = mix_tile(r)
        return jnp.where(lax.rem(m, tps) == 0, nb2 * m, nb2 * m - 1)

    def kv_next(r):
        m = mix_tile(r)
        return jnp.where(lax.rem(m, tps) == tps - 1, nb2 * m + nb2 - 1, nb2 * m + nb2)

    def cs_prev(r):
        i = lax.rem(mix_tile(r), tps)
        return jnp.maximum(nb2 * i - 1, 0)

    def cs_next(r):
        i = lax.rem(mix_tile(r), tps)
        return jnp.minimum(nb2 * i + nb2, nb2 * tps - 1)

    mix = lambda width, blk: pl.BlockSpec((MM_TM, width), lambda r: (mix_tile(r), blk))
    mrg = lambda width, blk: pl.BlockSpec((MM_TM, width), lambda r: (merge_tile(r), blk))
    per_layer = lambda shape: pl.BlockSpec((None,) + shape, lambda r: (layer,) + (0,) * len(shape))
    weight = lambda shape: pl.BlockSpec((None,) + shape, lambda r: (layer,) + (0,) * len(shape),
                                        pipeline_mode=pl.Buffered(1))
    kvb = COL_KVB // (2 * DKV_B)
    in_specs = [
        pl.BlockSpec(memory_space=pltpu.SMEM),
        mix(MIX_WIDTH, COL_MIX // MIX_WIDTH),
        pl.BlockSpec((BLOCK, 2 * DKV_B), lambda r: (kv_prev(r), kvb)),
        mix(2 * DKV_B, kvb),
        pl.BlockSpec((BLOCK, 2 * DKV_B), lambda r: (kv_next(r), kvb)),
        pl.BlockSpec((BLOCK, 2 * HD_B), lambda r: (cs_prev(r), 0)),
        pl.BlockSpec((MM_TM, 2 * HD_B), lambda r: (lax.rem(mix_tile(r), tps), 0)),
        pl.BlockSpec((BLOCK, 2 * HD_B), lambda r: (cs_next(r), 0)),
        per_layer((1, D_A)), per_layer((1, D_A)), per_layer((G_A, BLOCK, BLOCK)),
        per_layer((BLOCK, D_A)),
        mix(DV_C, 0), mix(DV_C, 0),
        mrg(3 * D_MODEL, COL_GATE // (3 * D_MODEL)),
        mrg(D_MODEL, 0),
        weight((D_A, D_MODEL)), weight((D_B, D_MODEL)), weight((DV_C, D_MODEL)),
        weight((D_MODEL, D_MODEL)), per_layer((1, HDV_C)),
        pl.BlockSpec((1, D_MODEL), lambda r: (0, 0)),
    ]
    h_dtype = F32 if last else BF16
    out_specs = [mrg(D_MODEL, 0)]
    out_shape = [jax.ShapeDtypeStruct((t, D_MODEL), h_dtype)]
    if not last:
        out_specs = [mrg(D_MODEL, 0)] + out_specs
        out_shape = [jax.ShapeDtypeStruct((t, D_MODEL), F32)] + out_shape
    outs = pl.pallas_call(
        functools.partial(_mixmerge_kernel, n_tiles=n_tiles, tiles_per_seq=tps, emit_x=not last),
        grid=(n_tiles + 1,),
        in_specs=in_specs,
        out_specs=out_specs,
        out_shape=out_shape,
        scratch_shapes=[pltpu.VMEM((2, MM_TM, D_A), BF16), pltpu.VMEM((2, MM_TM, D_B), BF16),
                        pltpu.VMEM((2, MM_TM, DV_C), BF16), pltpu.VMEM((MM_TM, D_MODEL), BF16),
                        pltpu.VMEM((MM_TM, D_MODEL), F32), pltpu.VMEM((MM_TM, 1), F32)],
        compiler_params=pltpu.CompilerParams(dimension_semantics=("arbitrary",),
                                             vmem_limit_bytes=VMEM_LIMIT),
        name="mixmerge",
    )(params["b_sink"][layer], proj, proj, proj, proj, cs, cs, cs,
      params["a_ln_g"], params["a_ln_b"], params["a_ws"], params["a_bias"],
      o_f, o_b, proj, x,
      params["w_pa"], params["w_pb"], params["w_pc"], params["w_out"], params["c_norm_g"], next_g)
    return (None, outs[0]) if last else (outs[0], outs[1])


def _rope_table(seq):
    half = HD_B // 2
    inv = ROPE_THETA ** (-jnp.arange(half, dtype=F32) * 2.0 / HD_B)
    ang = jnp.arange(seq, dtype=F32)[:, None] * inv[None, :]
    cos, sin = jnp.cos(ang), jnp.sin(ang)
    return jnp.concatenate([cos, cos, -sin, sin], axis=-1)


def _pad_gate_w(w, row0):
    z = jnp.zeros((w.shape[0], LANES, DK_C), w.dtype)
    return z.at[:, row0:row0 + GATE_RANK].set(w).astype(BF16)


def _prepare(norm_g, w_in, a_ln_g, a_ln_b, a_ws, a_bs, b_sink, c_wf, c_bf, c_wb, c_bb, c_norm_g,
             w_pa, w_pb, w_pc, w_out, final_g):
    return dict(
        norm_g=norm_g, final_g=final_g, w_in=_prep_w_in(w_in),
        a_ln_g=a_ln_g[:, None], a_ln_b=a_ln_b[:, None], a_ws=a_ws.astype(BF16),
        a_bias=jnp.repeat(jnp.swapaxes(a_bs, 1, 2), DG_A, axis=2),
        b_sink=b_sink,
        c_wf=_pad_gate_w(c_wf, 0), c_bf=c_bf, c_wb=_pad_gate_w(c_wb, GATE_RANK), c_bb=c_bb,
        c_norm_g=c_norm_g[:, None], w_pa=w_pa.astype(BF16), w_pb=w_pb.astype(BF16),
        w_pc=w_pc.astype(BF16), w_out=w_out.astype(BF16),
    )


def _trunk(x, params, batch, seq):
    t = batch * seq
    assert x.shape == (batch, seq, D_MODEL) and x.dtype == F32
    assert seq % GLA_TB == 0 and seq % MM_TM == 0 and t % PROJ_TM == 0, (batch, seq)
    x = x.reshape(t, D_MODEL)
    cs = _rope_table(seq)
    h = _rmsnorm(x, params["norm_g"][0][None])
    for l in range(DEPTH):
        last = l == DEPTH - 1
        proj = _proj(h, params["w_in"], l)
        o_f, o_b = _gla(proj, params["c_wf"][l], params["c_bf"][l][None], params["c_wb"][l],
                        params["c_bb"][l][None], batch, seq)
        next_g = params["final_g"] if last else params["norm_g"][l + 1]
        x, h = _mixmerge(proj, o_f, o_b, x, cs, params, l, next_g[None], batch, seq, last)
    return h.reshape(batch, seq, D_MODEL)


def kernel(x_prompt, x_sample, norm_g, w_in, a_ln_g, a_ln_b, a_ws, a_bs, b_sink, c_wf, c_bf,
           c_wb, c_bb, c_norm_g, w_pa, w_pb, w_pc, w_out, final_g):
    params = _prepare(norm_g, w_in, a_ln_g, a_ln_b, a_ws, a_bs, b_sink, c_wf, c_bf, c_wb, c_bb,
                      c_norm_g, w_pa, w_pb, w_pc, w_out, final_g)
    y_prompt = _trunk(x_prompt, params, x_prompt.shape[0], x_prompt.shape[1])
    y_sample = _trunk(x_sample, params, x_sample.shape[0], x_sample.shape[1])
    return (y_prompt, y_sample)
```

```python
import functools

import numpy as np
import jax
import jax.numpy as jnp
from jax import lax
from jax.experimental import pallas as pl
from jax.experimental.pallas import tpu as pltpu

F32 = jnp.float32
BF16 = jnp.bfloat16

D_MODEL = 2048
DEPTH = 4
EPS = 1e-6
BLOCK = 128
D_A = 1024
G_A = 4
DG_A = D_A // G_A
HQ_B = 8
HKV_B = 2
HD_B = 128
D_B = HQ_B * HD_B
DKV_B = HKV_B * HD_B
GQ_B = HQ_B // HKV_B
ROPE_THETA = 10000.0
H_C = 4
DK_C = 512
DV_C = 1024
HDK_C = DK_C // H_C
HDV_C = DV_C // H_C
GATE_RANK = 16
GATE_TEMP = 16.0
CHUNK_C = 64
IN_SPLITS = (D_A, D_A, D_A, D_B, DKV_B, DKV_B, D_B, DK_C, DK_C, DV_C, DV_C,
             GATE_RANK, GATE_RANK, D_MODEL, D_MODEL, D_MODEL)
N_IN = sum(IN_SPLITS)
SEG_ORDER = (0, 1, 2, 3, 6, 10, 13, 14, 15, 7, 8, 9, 4, 5, 11, 12)
HALVED_SEGS = (2, 6, 10, 13, 14, 15)

COL_MIX = 0
MIX_WIDTH = 3 * D_A + 2 * D_B + DV_C
COL_GATE = COL_MIX + MIX_WIDTH
COL_C = COL_GATE + 3 * D_MODEL
COL_KVB = COL_C + 2 * DK_C + DV_C
COL_LR = COL_KVB + 2 * DKV_B
LANES = 128
SUBLANES = 8
N_PROJ = 15360
C_WIDTH = 2 * DK_C + DV_C

PREP_TC = 256
PROJ_TM = 2048
PROJ0_TM = 1024
PROJ_TN = 1536
GLA_TB = 1024
GLA_CUM = 256
GLA_LAG = 9
MM_TM = 2 * BLOCK
MERGE_TN = 2048
NEG_BIG = -1e30
LOG2E = 1.4426950408889634
V7X_VMEM_BYTES = 64 * 1024 * 1024
VMEM_LIMIT = V7X_VMEM_BYTES * 7 // 8


def _silu_of_half(hz):
    return hz + hz * jnp.tanh(hz)


def _prep_segments():
    offs = np.concatenate([[0], np.cumsum(IN_SPLITS)])
    segs, dst = [], 0
    for i in SEG_ORDER:
        segs.append((dst, int(offs[i]), IN_SPLITS[i], 0.5 if i in HALVED_SEGS else 1.0))
        dst += IN_SPLITS[i]
    return segs, dst


def _prep_w_in_kernel(wt_ref, o_ref):
    segs, pad0 = _prep_segments()
    for dst, src, width, scale in segs:
        if width < LANES:
            continue
        w = wt_ref[src:src + width, :]
        if scale != 1.0:
            w = w * scale
        o_ref[:, dst:dst + width] = w.T.astype(o_ref.dtype)
    (dst_lr, src_lr, _, _), lr_width = segs[-2], 2 * GATE_RANK
    slab = wt_ref[src_lr:src_lr + LANES, :].T
    o_ref[:, dst_lr:dst_lr + lr_width] = slab[:, 0:lr_width].astype(o_ref.dtype)
    o_ref[:, pad0:] = jnp.zeros((o_ref.shape[0], N_PROJ - pad0), o_ref.dtype)


def _prep_w_in(w_in):
    depth = w_in.shape[0]
    return pl.pallas_call(
        _prep_w_in_kernel,
        grid=(depth, D_MODEL // PREP_TC),
        in_specs=[pl.BlockSpec((None, N_IN, PREP_TC), lambda l, i: (l, 0, i))],
        out_specs=pl.BlockSpec((None, PREP_TC, N_PROJ), lambda l, i: (l, i, 0)),
        out_shape=jax.ShapeDtypeStruct((depth, D_MODEL, N_PROJ), BF16),
        compiler_params=pltpu.CompilerParams(vmem_limit_bytes=VMEM_LIMIT),
        name="prep_w_in",
    )(jnp.swapaxes(w_in, 1, 2))


def _proj_kernel(h_ref, w_ref, o_ref):
    o_ref[...] = jnp.dot(h_ref[...], w_ref[...], preferred_element_type=F32).astype(o_ref.dtype)


def _proj(h, w_all, layer):
    t = h.shape[0]
    return pl.pallas_call(
        _proj_kernel,
        grid=(t // PROJ_TM, N_PROJ // PROJ_TN),
        in_specs=[pl.BlockSpec((PROJ_TM, D_MODEL), lambda i, j: (i, 0)),
                  pl.BlockSpec((None, D_MODEL, PROJ_TN), lambda i, j: (layer, 0, j))],
        out_specs=pl.BlockSpec((PROJ_TM, PROJ_TN), lambda i, j: (i, j)),
        out_shape=jax.ShapeDtypeStruct((t, N_PROJ), BF16),
        compiler_params=pltpu.CompilerParams(vmem_limit_bytes=VMEM_LIMIT),
        name="proj",
    )(h, w_all)


def _proj_norm_kernel(x_ref, g_ref, w_ref, o_ref, h_s):
    @pl.when(pl.program_id(1) == 0)
    def _():
        x = x_ref[...]
        h = x * lax.rsqrt(jnp.mean(x * x, axis=-1, keepdims=True) + EPS) * g_ref[...]
        h_s[...] = h.astype(h_s.dtype)

    o_ref[...] = jnp.dot(h_s[...], w_ref[...], preferred_element_type=F32).astype(o_ref.dtype)


def _proj_norm(x, g, w_all, layer):
    t = x.shape[0]
    return pl.pallas_call(
        _proj_norm_kernel,
        grid=(t // PROJ0_TM, N_PROJ // PROJ_TN),
        in_specs=[pl.BlockSpec((PROJ0_TM, D_MODEL), lambda i, j: (i, 0)),
                  pl.BlockSpec((1, D_MODEL), lambda i, j: (0, 0)),
                  pl.BlockSpec((None, D_MODEL, PROJ_TN), lambda i, j: (layer, 0, j))],
        out_specs=pl.BlockSpec((PROJ0_TM, PROJ_TN), lambda i, j: (i, j)),
        out_shape=jax.ShapeDtypeStruct((t, N_PROJ), BF16),
        scratch_shapes=[pltpu.VMEM((PROJ0_TM, D_MODEL), BF16)],
        compiler_params=pltpu.CompilerParams(dimension_semantics=("arbitrary", "arbitrary"),
                                             vmem_limit_bytes=VMEM_LIMIT),
        name="proj_norm",
    )(x, g, w_all)


GLA_GROUP = 2 * CHUNK_C


def _cum_matrix(reverse):
    r, c = np.indices((GLA_CUM, GLA_CUM))
    tri = (c >= r) if reverse else (c <= r)
    return jnp.asarray(((r // CHUNK_C == c // CHUNK_C) & tri).astype(np.float32), dtype=BF16)


def _gla_items(c_ref, lr_ref, w_ref, b_ref, cum_ref, o_ref, s_ref, *, reverse):
    nt = (((1,), (1,)), ((), ()))
    tn = (((0,), (0,)), ((), ()))
    last = 0 if reverse else CHUNK_C - 1
    n_groups = GLA_TB // GLA_GROUP
    order = list(range(n_groups - 1, -1, -1) if reverse else range(n_groups))
    env = {}

    def gates():
        x = jnp.dot(lr_ref[...], w_ref[...], preferred_element_type=F32) + b_ref[...]
        g = (jnp.minimum(x, 0.0) * (LOG2E / GATE_TEMP)
             - jnp.log2(1.0 + jnp.exp2(jnp.abs(x) * -LOG2E)) * (1.0 / GATE_TEMP))
        cum = cum_ref[...]
        g_hi = g.astype(BF16)
        g_lo = (g - g_hi.astype(F32)).astype(BF16)
        env["b"] = jnp.concatenate(
            [jnp.dot(cum, g_hi[i:i + GLA_CUM], preferred_element_type=F32)
             + jnp.dot(cum, g_lo[i:i + GLA_CUM], preferred_element_type=F32)
             for i in range(0, GLA_TB, GLA_CUM)], axis=0)
        env["s"] = [s_ref[h] for h in range(H_C)]

    def prep(gi):
        rows0 = slice(gi * GLA_GROUP, gi * GLA_GROUP + CHUNK_C)
        rows1 = slice(gi * GLA_GROUP + CHUNK_C, (gi + 1) * GLA_GROUP)
        b0, b1 = env["b"][rows0], env["b"][rows1]
        bl0, bl1 = b0[last:last + 1], b1[last:last + 1]
        dec0, dec1 = jnp.exp2(bl0), jnp.exp2(bl1)
        q0 = c_ref[rows0, 0:DK_C].astype(F32) * (HDK_C ** -0.5)
        q1 = c_ref[rows1, 0:DK_C].astype(F32) * (HDK_C ** -0.5)
        k0 = c_ref[rows0, DK_C:2 * DK_C].astype(F32)
        k1 = c_ref[rows1, DK_C:2 * DK_C].astype(F32)
        qe0, qe1 = q0 * jnp.exp2(b0), q1 * jnp.exp2(b1)
        ke0, ke1 = (k0 * jnp.exp2(-b0)).astype(BF16), (k1 * jnp.exp2(-b1)).astype(BF16)
        kd0, kd1 = k0 * jnp.exp2(bl0 - b0), k1 * jnp.exp2(bl1 - b1)
        if reverse:
            q_inter = jnp.concatenate([qe0 * dec1, qe1], axis=0).astype(BF16)
            k_state = jnp.concatenate([kd0, kd1 * dec0], axis=0).astype(BF16)
            keys0 = jnp.concatenate([ke0, kd1.astype(BF16)], axis=0)
            keys1 = jnp.concatenate([ke0, ke1], axis=0)
        else:
            q_inter = jnp.concatenate([qe0, qe1 * dec0], axis=0).astype(BF16)
            k_state = jnp.concatenate([kd0 * dec1, kd1], axis=0).astype(BF16)
            keys0 = jnp.concatenate([ke0, ke1], axis=0)
            keys1 = jnp.concatenate([kd0.astype(BF16), ke1], axis=0)
        env["prep", gi] = (qe0.astype(BF16), qe1.astype(BF16), keys0, keys1, q_inter, k_state,
                           dec0 * dec1)

    def intra(gi):
        qe0, qe1, keys0, keys1, q_inter, k_state, dec = env["prep", gi]
        rows = slice(gi * GLA_GROUP, (gi + 1) * GLA_GROUP)
        r2 = lax.broadcasted_iota(jnp.int32, (GLA_GROUP, GLA_GROUP), 0)
        c2 = lax.broadcasted_iota(jnp.int32, (GLA_GROUP, GLA_GROUP), 1)
        keep = (c2 > r2) if reverse else (c2 <= r2)
        per_head = []
        for h in range(H_C):
            kcols = slice(h * HDK_C, (h + 1) * HDK_C)
            v = c_ref[rows, 2 * DK_C + h * HDV_C:2 * DK_C + (h + 1) * HDV_C]
            a = jnp.concatenate(
                [lax.dot_general(qe0[:, kcols], keys0[:, kcols], nt, preferred_element_type=F32),
                 lax.dot_general(qe1[:, kcols], keys1[:, kcols], nt, preferred_element_type=F32)],
                axis=0)
            a = jnp.where(keep, a, 0.0).astype(BF16)
            ds = lax.dot_general(k_state[:, kcols], v, tn, preferred_element_type=F32)
            dec_col = jnp.transpose(jnp.broadcast_to(dec[:, kcols], (SUBLANES, HDK_C)))[:, 0:1]
            lhs = jnp.concatenate([a, q_inter[:, kcols]], axis=1)
            per_head.append((lhs, ds, dec_col))
        env["intra", gi] = per_head

    def scan(gi, final):
        rows = slice(gi * GLA_GROUP, (gi + 1) * GLA_GROUP)
        states = env["s"]
        for h, (lhs, ds, dec_col) in enumerate(env["intra", gi]):
            v = c_ref[rows, 2 * DK_C + h * HDV_C:2 * DK_C + (h + 1) * HDV_C]
            rhs = jnp.concatenate([v, states[h].astype(BF16)], axis=0)
            o = jnp.dot(lhs, rhs, preferred_element_type=F32)
            o_ref[rows, h * HDV_C:(h + 1) * HDV_C] = o.astype(o_ref.dtype)
            states[h] = dec_col * states[h] + ds
            if final:
                s_ref[h] = states[h]

    items = [gates]
    for gi in order:
        items += [functools.partial(prep, gi), functools.partial(intra, gi)]
    for n, gi in enumerate(order):
        items.append(functools.partial(scan, gi, n == len(order) - 1))
    return items


def _gla_kernel(cf_ref, lrf_ref, cb_ref, lrb_ref, wf_ref, bf_ref, wb_ref, bb_ref, cumf_ref,
                cumb_ref, of_ref, ob_ref, sf_ref, sb_ref):
    @pl.when(pl.program_id(1) == 0)
    def _():
        sf_ref[...] = jnp.zeros_like(sf_ref)
        sb_ref[...] = jnp.zeros_like(sb_ref)

    fwd = _gla_items(cf_ref, lrf_ref, wf_ref, bf_ref, cumf_ref, of_ref, sf_ref, reverse=False)
    bwd = _gla_items(cb_ref, lrb_ref, wb_ref, bb_ref, cumb_ref, ob_ref, sb_ref, reverse=True)
    lag = GLA_LAG
    for n in range(len(fwd) + lag):
        if n < len(fwd):
            fwd[n]()
        if 0 <= n - lag < len(bwd):
            bwd[n - lag]()


def _gla(proj, wf_pad, bf, wb_pad, bb, batch, seq):
    t = batch * seq
    nblk = seq // GLA_TB

    def fwd(b, j):
        return b * nblk + j

    def bwd(b, j):
        return b * nblk + (nblk - 1 - j)

    cblk = COL_C // C_WIDTH
    lrblk = COL_LR // LANES
    in_specs = [
        pl.BlockSpec((GLA_TB, C_WIDTH), lambda b, j: (fwd(b, j), cblk)),
        pl.BlockSpec((GLA_TB, LANES), lambda b, j: (fwd(b, j), lrblk)),
        pl.BlockSpec((GLA_TB, C_WIDTH), lambda b, j: (bwd(b, j), cblk)),
        pl.BlockSpec((GLA_TB, LANES), lambda b, j: (bwd(b, j), lrblk)),
        pl.BlockSpec((LANES, DK_C), lambda b, j: (0, 0)),
        pl.BlockSpec((1, DK_C), lambda b, j: (0, 0)),
        pl.BlockSpec((LANES, DK_C), lambda b, j: (0, 0)),
        pl.BlockSpec((1, DK_C), lambda b, j: (0, 0)),
        pl.BlockSpec((GLA_CUM, GLA_CUM), lambda b, j: (0, 0)),
        pl.BlockSpec((GLA_CUM, GLA_CUM), lambda b, j: (0, 0)),
    ]
    out_specs = [pl.BlockSpec((GLA_TB, DV_C), lambda b, j: (fwd(b, j), 0)),
                 pl.BlockSpec((GLA_TB, DV_C), lambda b, j: (bwd(b, j), 0))]
    return pl.pallas_call(
        _gla_kernel,
        grid=(batch, nblk),
        in_specs=in_specs,
        out_specs=out_specs,
        out_shape=[jax.ShapeDtypeStruct((t, DV_C), BF16), jax.ShapeDtypeStruct((t, DV_C), BF16)],
        scratch_shapes=[pltpu.VMEM((H_C, HDK_C, HDV_C), F32), pltpu.VMEM((H_C, HDK_C, HDV_C), F32)],
        compiler_params=pltpu.CompilerParams(dimension_semantics=("arbitrary", "arbitrary"),
                                             vmem_limit_bytes=VMEM_LIMIT),
        name="gla",
    )(proj, proj, proj, proj, wf_pad, bf, wb_pad, bb, _cum_matrix(False), _cum_matrix(True))


def _mix_a_items(a3_ref, lng_ref, lnb_ref, ws_ref, bias_ref, ya_ref):
    env = {}

    def norm(blk):
        rows = slice(blk * BLOCK, (blk + 1) * BLOCK)
        va = a3_ref[rows, D_A:2 * D_A].astype(F32)
        mu = jnp.mean(va, axis=-1, keepdims=True)
        dv = va - mu
        var = jnp.mean(dv * dv, axis=-1, keepdims=True)
        env[blk] = (dv * lax.rsqrt(var + EPS) * lng_ref[...] + lnb_ref[...]).astype(BF16)

    def group(blk, g):
        rows = slice(blk * BLOCK, (blk + 1) * BLOCK)
        cols = slice(g * DG_A, (g + 1) * DG_A)
        f = jnp.dot(ws_ref[g], env[blk][:, cols], preferred_element_type=F32) + bias_ref[:, cols]
        ua = a3_ref[rows, g * DG_A:(g + 1) * DG_A].astype(F32)
        za = a3_ref[rows, 2 * D_A + g * DG_A:2 * D_A + (g + 1) * DG_A].astype(F32)
        ya_ref[rows, cols] = (ua * f * _silu_of_half(za)).astype(ya_ref.dtype)

    items = []
    for blk in range(MM_TM // BLOCK):
        items.append(functools.partial(norm, blk))
        items += [functools.partial(group, blk, g) for g in range(G_A)]
    return items


def _mix_b_items(sink_ref, q_ref, zb_ref, kvp_ref, kvc_ref, kvn_ref, cs_ref, yb_ref, first, final):
    def rope(x, cs):
        return x * cs[:, :HD_B] + pltpu.roll(x, HD_B // 2, 1) * cs[:, HD_B:]

    q_scale = (HD_B ** -0.5) * LOG2E
    roped = {}
    env = {}

    def logits(kh, blk):
        kcol = slice(kh * HD_B, (kh + 1) * HD_B)
        vcol = slice(DKV_B + kh * HD_B, DKV_B + (kh + 1) * HD_B)
        if kh not in roped:
            cs_k = cs_ref[...]
            k_raw = jnp.concatenate([kvp_ref[:, kcol], kvc_ref[:, kcol], kvn_ref[:, kcol]], axis=0)
            k4 = rope(k_raw.astype(F32), cs_k).astype(BF16)
            v4 = jnp.concatenate([kvp_ref[:, vcol], kvc_ref[:, vcol], kvn_ref[:, vcol]], axis=0)
            roped[kh] = (k4, v4)
        k4, v4 = roped[kh]
        rows = slice(blk * BLOCK, (blk + 1) * BLOCK)
        if ("mask", blk) not in env:
            il = lax.broadcasted_iota(jnp.int32, (BLOCK, BLOCK), 0)
            jl = lax.broadcasted_iota(jnp.int32, (BLOCK, BLOCK), 1)
            has_prev = 1 if blk > 0 else 1 - first
            has_next = 1 if blk < MM_TM // BLOCK - 1 else 1 - final
            env["mask", blk] = (jnp.where(jl >= il + BLOCK * (1 - has_prev), 0.0, NEG_BIG),
                                jnp.where(jl <= il - BLOCK * (1 - has_next), 0.0, NEG_BIG))
        k3 = k4[blk * BLOCK:(blk + 3) * BLOCK]
        v3 = v4[blk * BLOCK:(blk + 3) * BLOCK]
        cs_q = cs_ref[(blk + 1) * BLOCK:(blk + 2) * BLOCK, :]
        qs = jnp.concatenate(
            [rope(q_ref[rows, (kh * GQ_B + g) * HD_B:(kh * GQ_B + g + 1) * HD_B].astype(F32),
                  cs_q) * q_scale for g in range(GQ_B)], axis=0).astype(BF16)
        s = lax.dot_general(qs, k3, (((1,), (1,)), ((), ())), preferred_element_type=F32)
        env[kh, blk] = (s, v3)

    def head(kh, blk, g):
        s, v3 = env[kh, blk]
        mask_prev, mask_next = env["mask", blk]
        rows = slice(blk * BLOCK, (blk + 1) * BLOCK)
        hq = kh * GQ_B + g
        sink = sink_ref[hq] * LOG2E
        sg = s[g * BLOCK:(g + 1) * BLOCK]
        sg = jnp.concatenate([sg[:, :BLOCK] + mask_prev, sg[:, BLOCK:2 * BLOCK],
                              sg[:, 2 * BLOCK:] + mask_next], axis=1)
        m = jnp.maximum(jnp.max(sg, axis=-1, keepdims=True), sink)
        p = jnp.exp2(sg - m)
        den = jnp.sum(p, axis=-1, keepdims=True) + jnp.exp2(sink - m)
        o = jnp.dot(p.astype(BF16), v3, preferred_element_type=F32) / den
        hc = slice(hq * HD_B, (hq + 1) * HD_B)
        yb_ref[rows, hc] = (o * _silu_of_half(zb_ref[rows, hc].astype(F32))).astype(yb_ref.dtype)

    items = []
    for kh in range(HKV_B):
        for blk in range(MM_TM // BLOCK):
            items.append(functools.partial(logits, kh, blk))
            items += [functools.partial(head, kh, blk, g) for g in range(GQ_B)]
    return items


def _mix_c_items(of_ref, ob_ref, zc_ref, cg_ref, yc_ref):
    def item(h):
        hc = slice(h * HDV_C, (h + 1) * HDV_C)
        oh = of_ref[:, hc].astype(F32) + ob_ref[:, hc].astype(F32)
        ms = jnp.mean(oh * oh, axis=-1, keepdims=True)
        yc_ref[:, hc] = (oh * lax.rsqrt(ms + EPS) * cg_ref[...]
                         * _silu_of_half(zc_ref[:, hc].astype(F32))).astype(yc_ref.dtype)

    return [functools.partial(item, h) for h in range(H_C)]


def _merge_items(ya_ref, yb_ref, yc_ref, ga_ref, gb_ref, gc_ref, x_ref,
                 wpa_ref, wpb_ref, wpc_ref, wout_ref, ng_ref, x_out_ref, h_ref,
                 mg_s, xn_s, ss_s):
    n_chunks = D_MODEL // MERGE_TN

    branches = ((ya_ref, wpa_ref, ga_ref), (yb_ref, wpb_ref, gb_ref), (yc_ref, wpc_ref, gc_ref))
    env = {}

    def branch(c, k):
        cols = slice(c * MERGE_TN, (c + 1) * MERGE_TN)
        y_ref, w_ref, g_ref = branches[k]
        part = ((1.0 + jnp.tanh(g_ref[:, cols].astype(F32)))
                * jnp.dot(y_ref[...], w_ref[:, cols], preferred_element_type=F32))
        acc = part if k == 0 else env[c] + part
        if k == len(branches) - 1:
            mg_s[:, cols] = (0.5 * acc).astype(mg_s.dtype)
        else:
            env[c] = acc

    def out(c):
        cols = slice(c * MERGE_TN, (c + 1) * MERGE_TN)
        xn = x_ref[:, cols] + jnp.dot(mg_s[...], wout_ref[:, cols], preferred_element_type=F32)
        xn_s[:, cols] = xn
        if x_out_ref is not None:
            x_out_ref[:, cols] = xn
        part = jnp.sum(xn * xn, axis=-1, keepdims=True)
        ss_s[...] = part if c == 0 else ss_s[...] + part

    def norm():
        hn = xn_s[...] * lax.rsqrt(ss_s[...] * (1.0 / D_MODEL) + EPS) * ng_ref[...]
        h_ref[...] = hn.astype(h_ref.dtype)

    return ([functools.partial(branch, c, k) for c in range(n_chunks) for k in range(len(branches))]
            + [functools.partial(out, c) for c in range(n_chunks)] + [norm])


def _mixmerge_kernel(sink_ref, m_ref, kvp_ref, kvc_ref, kvn_ref,
                     cs_ref, lng_ref, lnb_ref, ws_ref, bias_ref,
                     of_ref, ob_ref, g_ref, x_ref,
                     wpa_ref, wpb_ref, wpc_ref, wout_ref, cg_ref, ng_ref,
                     *rest, n_tiles, tiles_per_seq, emit_x):
    if emit_x:
        x_out_ref, h_ref, ya_s, yb_s, yc_s, mg_s, xn_s, ss_s = rest
    else:
        x_out_ref = None
        h_ref, ya_s, yb_s, yc_s, mg_s, xn_s, ss_s = rest
    a3_ref = m_ref.at[:, 0:3 * D_A]
    q_ref = m_ref.at[:, 3 * D_A:3 * D_A + D_B]
    zb_ref = m_ref.at[:, 3 * D_A + D_B:3 * D_A + 2 * D_B]
    zc_ref = m_ref.at[:, 3 * D_A + 2 * D_B:MIX_WIDTH]
    ga_ref, gb_ref, gc_ref = [g_ref.at[:, k * D_MODEL:(k + 1) * D_MODEL] for k in range(3)]
    r = pl.program_id(0)
    slot = lax.rem(r, 2)

    @pl.when(r == 0)
    def _():
        ya_s[1] = jnp.zeros(ya_s.shape[1:], ya_s.dtype)
        yb_s[1] = jnp.zeros(yb_s.shape[1:], yb_s.dtype)
        yc_s[1] = jnp.zeros(yc_s.shape[1:], yc_s.dtype)

    pos = lax.rem(jnp.minimum(r, n_tiles - 1), tiles_per_seq)
    first = (pos == 0).astype(jnp.int32)
    final = (pos == tiles_per_seq - 1).astype(jnp.int32)
    mix_a = _mix_a_items(a3_ref, lng_ref, lnb_ref, ws_ref, bias_ref, ya_s.at[slot])
    mix_b = _mix_b_items(sink_ref, q_ref, zb_ref, kvp_ref, kvc_ref, kvn_ref, cs_ref,
                         yb_s.at[slot], first, final)
    mix_c = _mix_c_items(of_ref, ob_ref, zc_ref, cg_ref, yc_s.at[slot])
    merge = _merge_items(ya_s.at[1 - slot], yb_s.at[1 - slot], yc_s.at[1 - slot], ga_ref, gb_ref,
                         gc_ref, x_ref, wpa_ref, wpb_ref, wpc_ref, wout_ref, ng_ref, x_out_ref,
                         h_ref, mg_s, xn_s, ss_s)
    mix = mix_c[:2] + mix_b[:10] + mix_a[:5] + mix_c[2:] + mix_b[10:] + mix_a[5:]
    slots = len(merge) - 1
    done = 0
    for n, merge_item in enumerate(merge):
        merge_item()
        upto = len(mix) if n >= slots - 1 else (len(mix) * (n + 1)) // slots
        for mix_item in mix[done:upto]:
            mix_item()
        done = upto


def _mixmerge(proj, o_f, o_b, x, cs, params, layer, next_g, batch, seq, last):
    t = batch * seq
    n_tiles = t // MM_TM
    tps = seq // MM_TM
    nb2 = MM_TM // BLOCK

    def mix_tile(r):
        return jnp.minimum(r, n_tiles - 1)

    def merge_tile(r):
        return jnp.maximum(r - 1, 0)

    def kv_prev(r):
        m = mix_tile(r)
        return jnp.where(lax.rem(m, tps) == 0, nb2 * m, nb2 * m - 1)

    def kv_next(r):
        m = mix_tile(r)
        return jnp.where(lax.rem(m, tps) == tps - 1, nb2 * m + nb2 - 1, nb2 * m + nb2)

    mix = lambda width, blk: pl.BlockSpec((MM_TM, width), lambda r: (mix_tile(r), blk))
    mrg = lambda width, blk: pl.BlockSpec((MM_TM, width), lambda r: (merge_tile(r), blk))
    per_layer = lambda shape: pl.BlockSpec((None,) + shape, lambda r: (layer,) + (0,) * len(shape))
    weight = lambda shape: pl.BlockSpec((None,) + shape, lambda r: (layer,) + (0,) * len(shape),
                                        pipeline_mode=pl.Buffered(1))
    kvb = COL_KVB // (2 * DKV_B)
    in_specs = [
        pl.BlockSpec(memory_space=pltpu.SMEM),
        mix(MIX_WIDTH, COL_MIX // MIX_WIDTH),
        pl.BlockSpec((BLOCK, 2 * DKV_B), lambda r: (kv_prev(r), kvb)),
        mix(2 * DKV_B, kvb),
        pl.BlockSpec((BLOCK, 2 * DKV_B), lambda r: (kv_next(r), kvb)),
        pl.BlockSpec((None, MM_TM + 2 * BLOCK, 2 * HD_B),
                     lambda r: (lax.rem(mix_tile(r), tps), 0, 0)),
        per_layer((1, D_A)), per_layer((1, D_A)), per_layer((G_A, BLOCK, BLOCK)),
        per_layer((BLOCK, D_A)),
        mix(DV_C, 0), mix(DV_C, 0),
        mrg(3 * D_MODEL, COL_GATE // (3 * D_MODEL)),
        mrg(D_MODEL, 0),
        weight((D_A, D_MODEL)), weight((D_B, D_MODEL)), weight((DV_C, D_MODEL)),
        weight((D_MODEL, D_MODEL)), per_layer((1, HDV_C)),
        pl.BlockSpec((1, D_MODEL), lambda r: (0, 0)),
    ]
    h_dtype = F32 if last else BF16
    out_specs = [mrg(D_MODEL, 0)]
    out_shape = [jax.ShapeDtypeStruct((t, D_MODEL), h_dtype)]
    if not last:
        out_specs = [mrg(D_MODEL, 0)] + out_specs
        out_shape = [jax.ShapeDtypeStruct((t, D_MODEL), F32)] + out_shape
    outs = pl.pallas_call(
        functools.partial(_mixmerge_kernel, n_tiles=n_tiles, tiles_per_seq=tps, emit_x=not last),
        grid=(n_tiles + 1,),
        in_specs=in_specs,
        out_specs=out_specs,
        out_shape=out_shape,
        scratch_shapes=[pltpu.VMEM((2, MM_TM, D_A), BF16), pltpu.VMEM((2, MM_TM, D_B), BF16),
                        pltpu.VMEM((2, MM_TM, DV_C), BF16), pltpu.VMEM((MM_TM, D_MODEL), BF16),
                        pltpu.VMEM((MM_TM, D_MODEL), F32), pltpu.VMEM((MM_TM, 1), F32)],
        compiler_params=pltpu.CompilerParams(dimension_semantics=("arbitrary",),
                                             vmem_limit_bytes=VMEM_LIMIT),
        name="mixmerge",
    )(params["b_sink"][layer], proj, proj, proj, proj, cs,
      params["a_ln_g"], params["a_ln_b"], params["a_ws"], params["a_bias"],
      o_f, o_b, proj, x,
      params["w_pa"], params["w_pb"], params["w_pc"], params["w_out"], params["c_norm_g"], next_g)
    return (None, outs[0]) if last else (outs[0], outs[1])


def _rope_table(seq):
    half = HD_B // 2
    inv = ROPE_THETA ** (-jnp.arange(half, dtype=F32) * 2.0 / HD_B)
    ang = jnp.arange(seq, dtype=F32)[:, None] * inv[None, :]
    cos, sin = jnp.cos(ang), jnp.sin(ang)
    table = jnp.concatenate([cos, cos, -sin, sin], axis=-1)
    padded = jnp.pad(table, ((BLOCK, BLOCK), (0, 0)))
    tps = seq // MM_TM
    lead = padded[:seq].reshape(tps, MM_TM, 2 * HD_B)
    tail = padded[MM_TM:MM_TM + seq].reshape(tps, MM_TM, 2 * HD_B)[:, :2 * BLOCK]
    return jnp.concatenate([lead, tail], axis=1)


def _pad_gate_w(w, row0):
    z = jnp.zeros((w.shape[0], LANES, DK_C), w.dtype)
    return z.at[:, row0:row0 + GATE_RANK].set(w).astype(BF16)


def _prepare(norm_g, w_in, a_ln_g, a_ln_b, a_ws, a_bs, b_sink, c_wf, c_bf, c_wb, c_bb, c_norm_g,
             w_pa, w_pb, w_pc, w_out, final_g):
    return dict(
        norm_g=norm_g, final_g=final_g, w_in=_prep_w_in(w_in),
        a_ln_g=a_ln_g[:, None], a_ln_b=a_ln_b[:, None], a_ws=a_ws.astype(BF16),
        a_bias=jnp.repeat(jnp.swapaxes(a_bs, 1, 2), DG_A, axis=2),
        b_sink=b_sink,
        c_wf=_pad_gate_w(c_wf, 0), c_bf=c_bf, c_wb=_pad_gate_w(c_wb, GATE_RANK), c_bb=c_bb,
        c_norm_g=c_norm_g[:, None], w_pa=w_pa.astype(BF16), w_pb=w_pb.astype(BF16),
        w_pc=w_pc.astype(BF16), w_out=w_out.astype(BF16),
    )


def _trunk(x, params, batch, seq):
    t = batch * seq
    assert x.shape == (batch, seq, D_MODEL) and x.dtype == F32
    assert seq % GLA_TB == 0 and seq % MM_TM == 0 and t % PROJ_TM == 0, (batch, seq)
    x = x.reshape(t, D_MODEL)
    cs = _rope_table(seq)
    h = None
    for l in range(DEPTH):
        last = l == DEPTH - 1
        if l == 0:
            proj = _proj_norm(x, params["norm_g"][0][None], params["w_in"], l)
        else:
            proj = _proj(h, params["w_in"], l)
        o_f, o_b = _gla(proj, params["c_wf"][l], params["c_bf"][l][None], params["c_wb"][l],
                        params["c_bb"][l][None], batch, seq)
        next_g = params["final_g"] if last else params["norm_g"][l + 1]
        x, h = _mixmerge(proj, o_f, o_b, x, cs, params, l, next_g[None], batch, seq, last)
    return h.reshape(batch, seq, D_MODEL)


def kernel(x_prompt, x_sample, norm_g, w_in, a_ln_g, a_ln_b, a_ws, a_bs, b_sink, c_wf, c_bf,
           c_wb, c_bb, c_norm_g, w_pa, w_pb, w_pc, w_out, final_g):
    params = _prepare(norm_g, w_in, a_ln_g, a_ln_b, a_ws, a_bs, b_sink, c_wf, c_bf, c_wb, c_bb,
                      c_norm_g, w_pa, w_pb, w_pc, w_out, final_g)
    y_prompt = _trunk(x_prompt, params, x_prompt.shape[0], x_prompt.shape[1])
    y_sample = _trunk(x_sample, params, x_sample.shape[0], x_sample.shape[1])
    return (y_prompt, y_sample)
```

```python
import functools

import numpy as np
import jax
import jax.numpy as jnp
from jax import lax
from jax.experimental import pallas as pl
from jax.experimental.pallas import tpu as pltpu

F32 = jnp.float32
BF16 = jnp.bfloat16

D_MODEL = 2048
DEPTH = 4
EPS = 1e-6
BLOCK = 128
D_A = 1024
G_A = 4
DG_A = D_A // G_A
HQ_B = 8
HKV_B = 2
HD_B = 128
D_B = HQ_B * HD_B
DKV_B = HKV_B * HD_B
GQ_B = HQ_B // HKV_B
ROPE_THETA = 10000.0
H_C = 4
DK_C = 512
DV_C = 1024
HDK_C = DK_C // H_C
HDV_C = DV_C // H_C
GATE_RANK = 16
GATE_TEMP = 16.0
CHUNK_C = 64
IN_SPLITS = (D_A, D_A, D_A, D_B, DKV_B, DKV_B, D_B, DK_C, DK_C, DV_C, DV_C,
             GATE_RANK, GATE_RANK, D_MODEL, D_MODEL, D_MODEL)
N_IN = sum(IN_SPLITS)
SEG_ORDER = (0, 1, 2, 3, 6, 10, 13, 14, 15, 7, 8, 9, 4, 5, 11, 12)
HALVED_SEGS = (2, 6, 10, 13, 14, 15)

COL_MIX = 0
MIX_WIDTH = 3 * D_A + 2 * D_B + DV_C
COL_GATE = COL_MIX + MIX_WIDTH
COL_C = COL_GATE + 3 * D_MODEL
COL_KVB = COL_C + 2 * DK_C + DV_C
COL_LR = COL_KVB + 2 * DKV_B
LANES = 128
SUBLANES = 8
N_PROJ = 15360
C_WIDTH = 2 * DK_C + DV_C

PREP_TC = 256
PROJ_TM = 2048
PROJ0_TM = 1024
PROJ_TN = 1536
GLA_TB = 1024
GLA_CUM = 256
GLA_LAG = 9
MM_TM = 2 * BLOCK
MERGE_TN = 2048
NEG_BIG = -1e30
LOG2E = 1.4426950408889634
V7X_VMEM_BYTES = 64 * 1024 * 1024
VMEM_LIMIT = V7X_VMEM_BYTES * 7 // 8


def _silu_of_half(hz):
    return hz + hz * jnp.tanh(hz)


def _prep_segments():
    offs = np.concatenate([[0], np.cumsum(IN_SPLITS)])
    segs, dst = [], 0
    for i in SEG_ORDER:
        segs.append((dst, int(offs[i]), IN_SPLITS[i], 0.5 if i in HALVED_SEGS else 1.0))
        dst += IN_SPLITS[i]
    return segs, dst


def _prep_w_in_kernel(wt_ref, o_ref):
    segs, pad0 = _prep_segments()
    for dst, src, width, scale in segs:
        if width < LANES:
            continue
        w = wt_ref[src:src + width, :]
        if scale != 1.0:
            w = w * scale
        o_ref[:, dst:dst + width] = w.T.astype(o_ref.dtype)
    (dst_lr, src_lr, _, _), lr_width = segs[-2], 2 * GATE_RANK
    slab = wt_ref[src_lr:src_lr + LANES, :].T
    o_ref[:, dst_lr:dst_lr + lr_width] = slab[:, 0:lr_width].astype(o_ref.dtype)
    o_ref[:, pad0:] = jnp.zeros((o_ref.shape[0], N_PROJ - pad0), o_ref.dtype)


def _prep_w_in(w_in):
    depth = w_in.shape[0]
    return pl.pallas_call(
        _prep_w_in_kernel,
        grid=(depth, D_MODEL // PREP_TC),
        in_specs=[pl.BlockSpec((None, N_IN, PREP_TC), lambda l, i: (l, 0, i))],
        out_specs=pl.BlockSpec((None, PREP_TC, N_PROJ), lambda l, i: (l, i, 0)),
        out_shape=jax.ShapeDtypeStruct((depth, D_MODEL, N_PROJ), BF16),
        compiler_params=pltpu.CompilerParams(vmem_limit_bytes=VMEM_LIMIT),
        name="prep_w_in",
    )(jnp.swapaxes(w_in, 1, 2))


def _proj_kernel(h_ref, w_ref, o_ref):
    o_ref[...] = jnp.dot(h_ref[...], w_ref[...], preferred_element_type=F32).astype(o_ref.dtype)


def _proj(h, w_all, layer):
    t = h.shape[0]
    return pl.pallas_call(
        _proj_kernel,
        grid=(t // PROJ_TM, N_PROJ // PROJ_TN),
        in_specs=[pl.BlockSpec((PROJ_TM, D_MODEL), lambda i, j: (i, 0)),
                  pl.BlockSpec((None, D_MODEL, PROJ_TN), lambda i, j: (layer, 0, j))],
        out_specs=pl.BlockSpec((PROJ_TM, PROJ_TN), lambda i, j: (i, j)),
        out_shape=jax.ShapeDtypeStruct((t, N_PROJ), BF16),
        compiler_params=pltpu.CompilerParams(vmem_limit_bytes=VMEM_LIMIT),
        name="proj",
    )(h, w_all)


def _proj_norm_kernel(x_ref, g_ref, w_ref, o_ref, h_s):
    @pl.when(pl.program_id(1) == 0)
    def _():
        x = x_ref[...]
        h = x * lax.rsqrt(jnp.mean(x * x, axis=-1, keepdims=True) + EPS) * g_ref[...]
        h_s[...] = h.astype(h_s.dtype)

    o_ref[...] = jnp.dot(h_s[...], w_ref[...], preferred_element_type=F32).astype(o_ref.dtype)


def _proj_norm(x, g, w_all, layer):
    t = x.shape[0]
    return pl.pallas_call(
        _proj_norm_kernel,
        grid=(t // PROJ0_TM, N_PROJ // PROJ_TN),
        in_specs=[pl.BlockSpec((PROJ0_TM, D_MODEL), lambda i, j: (i, 0)),
                  pl.BlockSpec((1, D_MODEL), lambda i, j: (0, 0)),
                  pl.BlockSpec((None, D_MODEL, PROJ_TN), lambda i, j: (layer, 0, j))],
        out_specs=pl.BlockSpec((PROJ0_TM, PROJ_TN), lambda i, j: (i, j)),
        out_shape=jax.ShapeDtypeStruct((t, N_PROJ), BF16),
        scratch_shapes=[pltpu.VMEM((PROJ0_TM, D_MODEL), BF16)],
        compiler_params=pltpu.CompilerParams(dimension_semantics=("arbitrary", "arbitrary"),
                                             vmem_limit_bytes=VMEM_LIMIT),
        name="proj_norm",
    )(x, g, w_all)


GLA_GROUP = 2 * CHUNK_C


def _cum_matrix(reverse):
    r, c = np.indices((GLA_CUM, GLA_CUM))
    tri = (c >= r) if reverse else (c <= r)
    return jnp.asarray(((r // CHUNK_C == c // CHUNK_C) & tri).astype(np.float32), dtype=BF16)


def _gla_items(c_ref, lr_ref, w_ref, b_ref, cum_ref, o_ref, s_ref, *, reverse):
    nt = (((1,), (1,)), ((), ()))
    tn = (((0,), (0,)), ((), ()))
    last = 0 if reverse else CHUNK_C - 1
    n_groups = GLA_TB // GLA_GROUP
    order = list(range(n_groups - 1, -1, -1) if reverse else range(n_groups))
    env = {}

    def gates():
        x = jnp.dot(lr_ref[...], w_ref[...], preferred_element_type=F32) + b_ref[...]
        g = (jnp.minimum(x, 0.0) * (LOG2E / GATE_TEMP)
             - jnp.log2(1.0 + jnp.exp2(jnp.abs(x) * -LOG2E)) * (1.0 / GATE_TEMP))
        cum = cum_ref[...]
        g_hi = g.astype(BF16)
        g_lo = (g - g_hi.astype(F32)).astype(BF16)
        env["b"] = jnp.concatenate(
            [jnp.dot(cum, g_hi[i:i + GLA_CUM], preferred_element_type=F32)
             + jnp.dot(cum, g_lo[i:i + GLA_CUM], preferred_element_type=F32)
             for i in range(0, GLA_TB, GLA_CUM)], axis=0)
        env["s"] = [s_ref[h] for h in range(H_C)]

    def prep(gi):
        rows0 = slice(gi * GLA_GROUP, gi * GLA_GROUP + CHUNK_C)
        rows1 = slice(gi * GLA_GROUP + CHUNK_C, (gi + 1) * GLA_GROUP)
        b0, b1 = env["b"][rows0], env["b"][rows1]
        bl0, bl1 = b0[last:last + 1], b1[last:last + 1]
        dec0, dec1 = jnp.exp2(bl0), jnp.exp2(bl1)
        q0 = c_ref[rows0, 0:DK_C].astype(F32) * (HDK_C ** -0.5)
        q1 = c_ref[rows1, 0:DK_C].astype(F32) * (HDK_C ** -0.5)
        k0 = c_ref[rows0, DK_C:2 * DK_C].astype(F32)
        k1 = c_ref[rows1, DK_C:2 * DK_C].astype(F32)
        qe0, qe1 = q0 * jnp.exp2(b0), q1 * jnp.exp2(b1)
        ke0, ke1 = (k0 * jnp.exp2(-b0)).astype(BF16), (k1 * jnp.exp2(-b1)).astype(BF16)
        kd0, kd1 = k0 * jnp.exp2(bl0 - b0), k1 * jnp.exp2(bl1 - b1)
        if reverse:
            q_inter = jnp.concatenate([qe0 * dec1, qe1], axis=0).astype(BF16)
            k_state = jnp.concatenate([kd0, kd1 * dec0], axis=0).astype(BF16)
            keys0 = jnp.concatenate([ke0, kd1.astype(BF16)], axis=0)
            keys1 = jnp.concatenate([ke0, ke1], axis=0)
        else:
            q_inter = jnp.concatenate([qe0, qe1 * dec0], axis=0).astype(BF16)
            k_state = jnp.concatenate([kd0 * dec1, kd1], axis=0).astype(BF16)
            keys0 = jnp.concatenate([ke0, ke1], axis=0)
            keys1 = jnp.concatenate([kd0.astype(BF16), ke1], axis=0)
        env["prep", gi] = (qe0.astype(BF16), qe1.astype(BF16), keys0, keys1, q_inter, k_state,
                           dec0 * dec1)

    def intra(gi):
        qe0, qe1, keys0, keys1, q_inter, k_state, dec = env["prep", gi]
        rows = slice(gi * GLA_GROUP, (gi + 1) * GLA_GROUP)
        r2 = lax.broadcasted_iota(jnp.int32, (GLA_GROUP, GLA_GROUP), 0)
        c2 = lax.broadcasted_iota(jnp.int32, (GLA_GROUP, GLA_GROUP), 1)
        keep = (c2 > r2) if reverse else (c2 <= r2)
        per_head = []
        for h in range(H_C):
            kcols = slice(h * HDK_C, (h + 1) * HDK_C)
            v = c_ref[rows, 2 * DK_C + h * HDV_C:2 * DK_C + (h + 1) * HDV_C]
            a = jnp.concatenate(
                [lax.dot_general(qe0[:, kcols], keys0[:, kcols], nt, preferred_element_type=F32),
                 lax.dot_general(qe1[:, kcols], keys1[:, kcols], nt, preferred_element_type=F32)],
                axis=0)
            a = jnp.where(keep, a, 0.0).astype(BF16)
            ds = lax.dot_general(k_state[:, kcols], v, tn, preferred_element_type=F32)
            dec_col = jnp.transpose(jnp.broadcast_to(dec[:, kcols], (SUBLANES, HDK_C)))[:, 0:1]
            lhs = jnp.concatenate([a, q_inter[:, kcols]], axis=1)
            per_head.append((lhs, ds, dec_col))
        env["intra", gi] = per_head

    def scan(gi, final):
        rows = slice(gi * GLA_GROUP, (gi + 1) * GLA_GROUP)
        states = env["s"]
        for h, (lhs, ds, dec_col) in enumerate(env["intra", gi]):
            v = c_ref[rows, 2 * DK_C + h * HDV_C:2 * DK_C + (h + 1) * HDV_C]
            rhs = jnp.concatenate([v, states[h].astype(BF16)], axis=0)
            o = jnp.dot(lhs, rhs, preferred_element_type=F32)
            o_ref[rows, h * HDV_C:(h + 1) * HDV_C] = o.astype(o_ref.dtype)
            states[h] = dec_col * states[h] + ds
            if final:
                s_ref[h] = states[h]

    items = [gates]
    for gi in order:
        items += [functools.partial(prep, gi), functools.partial(intra, gi)]
    for n, gi in enumerate(order):
        items.append(functools.partial(scan, gi, n == len(order) - 1))
    return items


def _gla_kernel(cf_ref, lrf_ref, cb_ref, lrb_ref, wf_ref, bf_ref, wb_ref, bb_ref, cumf_ref,
                cumb_ref, of_ref, ob_ref, sf_ref, sb_ref):
    @pl.when(pl.program_id(1) == 0)
    def _():
        sf_ref[...] = jnp.zeros_like(sf_ref)
        sb_ref[...] = jnp.zeros_like(sb_ref)

    fwd = _gla_items(cf_ref, lrf_ref, wf_ref, bf_ref, cumf_ref, of_ref, sf_ref, reverse=False)
    bwd = _gla_items(cb_ref, lrb_ref, wb_ref, bb_ref, cumb_ref, ob_ref, sb_ref, reverse=True)
    lag = GLA_LAG
    for n in range(len(fwd) + lag):
        if n < len(fwd):
            fwd[n]()
        if 0 <= n - lag < len(bwd):
            bwd[n - lag]()


def _gla(proj, wf_pad, bf, wb_pad, bb, batch, seq):
    t = batch * seq
    nblk = seq // GLA_TB

    def fwd(b, j):
        return b * nblk + j

    def bwd(b, j):
        return b * nblk + (nblk - 1 - j)

    cblk = COL_C // C_WIDTH
    lrblk = COL_LR // LANES
    in_specs = [
        pl.BlockSpec((GLA_TB, C_WIDTH), lambda b, j: (fwd(b, j), cblk)),
        pl.BlockSpec((GLA_TB, LANES), lambda b, j: (fwd(b, j), lrblk)),
        pl.BlockSpec((GLA_TB, C_WIDTH), lambda b, j: (bwd(b, j), cblk)),
        pl.BlockSpec((GLA_TB, LANES), lambda b, j: (bwd(b, j), lrblk)),
        pl.BlockSpec((LANES, DK_C), lambda b, j: (0, 0)),
        pl.BlockSpec((1, DK_C), lambda b, j: (0, 0)),
        pl.BlockSpec((LANES, DK_C), lambda b, j: (0, 0)),
        pl.BlockSpec((1, DK_C), lambda b, j: (0, 0)),
        pl.BlockSpec((GLA_CUM, GLA_CUM), lambda b, j: (0, 0)),
        pl.BlockSpec((GLA_CUM, GLA_CUM), lambda b, j: (0, 0)),
    ]
    out_specs = [pl.BlockSpec((GLA_TB, DV_C), lambda b, j: (fwd(b, j), 0)),
                 pl.BlockSpec((GLA_TB, DV_C), lambda b, j: (bwd(b, j), 0))]
    return pl.pallas_call(
        _gla_kernel,
        grid=(batch, nblk),
        in_specs=in_specs,
        out_specs=out_specs,
        out_shape=[jax.ShapeDtypeStruct((t, DV_C), BF16), jax.ShapeDtypeStruct((t, DV_C), BF16)],
        scratch_shapes=[pltpu.VMEM((H_C, HDK_C, HDV_C), F32), pltpu.VMEM((H_C, HDK_C, HDV_C), F32)],
        compiler_params=pltpu.CompilerParams(dimension_semantics=("arbitrary", "arbitrary"),
                                             vmem_limit_bytes=VMEM_LIMIT),
        name="gla",
    )(proj, proj, proj, proj, wf_pad, bf, wb_pad, bb, _cum_matrix(False), _cum_matrix(True))


def _mix_a_items(a3_ref, lng_ref, lnb_ref, ws_ref, bias_ref, ya_ref):
    env = {}

    def norm(blk):
        rows = slice(blk * BLOCK, (blk + 1) * BLOCK)
        va = a3_ref[rows, D_A:2 * D_A].astype(F32)
        mu = jnp.mean(va, axis=-1, keepdims=True)
        dv = va - mu
        var = jnp.mean(dv * dv, axis=-1, keepdims=True)
        env[blk] = (dv * lax.rsqrt(var + EPS) * lng_ref[...] + lnb_ref[...]).astype(BF16)

    def group(blk, g):
        rows = slice(blk * BLOCK, (blk + 1) * BLOCK)
        cols = slice(g * DG_A, (g + 1) * DG_A)
        f = jnp.dot(ws_ref[g], env[blk][:, cols], preferred_element_type=F32) + bias_ref[:, cols]
        ua = a3_ref[rows, g * DG_A:(g + 1) * DG_A].astype(F32)
        za = a3_ref[rows, 2 * D_A + g * DG_A:2 * D_A + (g + 1) * DG_A].astype(F32)
        ya_ref[rows, cols] = (ua * f * _silu_of_half(za)).astype(ya_ref.dtype)

    items = []
    for blk in range(MM_TM // BLOCK):
        items.append(functools.partial(norm, blk))
        items += [functools.partial(group, blk, g) for g in range(G_A)]
    return items


def _mix_b_items(sink_ref, q_ref, zb_ref, kvp_ref, kvc_ref, kvn_ref, cs_ref, yb_ref, first, final):
    def rope(x, cs):
        return x * cs[:, :HD_B] + pltpu.roll(x, HD_B // 2, 1) * cs[:, HD_B:]

    q_scale = (HD_B ** -0.5) * LOG2E
    roped = {}
    env = {}

    def logits(kh, blk):
        kcol = slice(kh * HD_B, (kh + 1) * HD_B)
        vcol = slice(DKV_B + kh * HD_B, DKV_B + (kh + 1) * HD_B)
        if kh not in roped:
            cs_k = cs_ref[...]
            k_raw = jnp.concatenate([kvp_ref[:, kcol], kvc_ref[:, kcol], kvn_ref[:, kcol]], axis=0)
            k4 = rope(k_raw.astype(F32), cs_k).astype(BF16)
            v4 = jnp.concatenate([kvp_ref[:, vcol], kvc_ref[:, vcol], kvn_ref[:, vcol]], axis=0)
            roped[kh] = (k4, v4)
        k4, v4 = roped[kh]
        rows = slice(blk * BLOCK, (blk + 1) * BLOCK)
        if ("mask", blk) not in env:
            il = lax.broadcasted_iota(jnp.int32, (BLOCK, BLOCK), 0)
            jl = lax.broadcasted_iota(jnp.int32, (BLOCK, BLOCK), 1)
            has_prev = 1 if blk > 0 else 1 - first
            has_next = 1 if blk < MM_TM // BLOCK - 1 else 1 - final
            env["mask", blk] = (jnp.where(jl >= il + BLOCK * (1 - has_prev), 0.0, NEG_BIG),
                                jnp.where(jl <= il - BLOCK * (1 - has_next), 0.0, NEG_BIG))
        k3 = k4[blk * BLOCK:(blk + 3) * BLOCK]
        v3 = v4[blk * BLOCK:(blk + 3) * BLOCK]
        cs_q = cs_ref[(blk + 1) * BLOCK:(blk + 2) * BLOCK, :]
        qs = jnp.concatenate(
            [rope(q_ref[rows, (kh * GQ_B + g) * HD_B:(kh * GQ_B + g + 1) * HD_B].astype(F32),
                  cs_q) * q_scale for g in range(GQ_B)], axis=0).astype(BF16)
        s = lax.dot_general(qs, k3, (((1,), (1,)), ((), ())), preferred_element_type=F32)
        env[kh, blk] = (s, v3)

    def head(kh, blk, g):
        s, v3 = env[kh, blk]
        mask_prev, mask_next = env["mask", blk]
        rows = slice(blk * BLOCK, (blk + 1) * BLOCK)
        hq = kh * GQ_B + g
        sink = sink_ref[hq] * LOG2E
        sg = s[g * BLOCK:(g + 1) * BLOCK]
        sg = jnp.concatenate([sg[:, :BLOCK] + mask_prev, sg[:, BLOCK:2 * BLOCK],
                              sg[:, 2 * BLOCK:] + mask_next], axis=1)
        m = jnp.maximum(jnp.max(sg, axis=-1, keepdims=True), sink)
        p = jnp.exp2(sg - m)
        den = jnp.sum(p, axis=-1, keepdims=True) + jnp.exp2(sink - m)
        o = jnp.dot(p.astype(BF16), v3, preferred_element_type=F32) / den
        hc = slice(hq * HD_B, (hq + 1) * HD_B)
        yb_ref[rows, hc] = (o * _silu_of_half(zb_ref[rows, hc].astype(F32))).astype(yb_ref.dtype)

    items = []
    for kh in range(HKV_B):
        for blk in range(MM_TM // BLOCK):
            items.append(functools.partial(logits, kh, blk))
            items += [functools.partial(head, kh, blk, g) for g in range(GQ_B)]
    return items


def _mix_c_items(of_ref, ob_ref, zc_ref, cg_ref, yc_ref):
    def item(h):
        hc = slice(h * HDV_C, (h + 1) * HDV_C)
        oh = of_ref[:, hc].astype(F32) + ob_ref[:, hc].astype(F32)
        ms = jnp.mean(oh * oh, axis=-1, keepdims=True)
        yc_ref[:, hc] = (oh * lax.rsqrt(ms + EPS) * cg_ref[...]
                         * _silu_of_half(zc_ref[:, hc].astype(F32))).astype(yc_ref.dtype)

    return [functools.partial(item, h) for h in range(H_C)]


def _merge_items(ya_ref, yb_ref, yc_ref, ga_ref, gb_ref, gc_ref, x_ref,
                 wpa_ref, wpb_ref, wpc_ref, wout_ref, ng_ref, x_out_ref, h_ref,
                 mg_s, xn_s, ss_s):
    n_chunks = D_MODEL // MERGE_TN

    branches = ((ya_ref, wpa_ref, ga_ref), (yb_ref, wpb_ref, gb_ref), (yc_ref, wpc_ref, gc_ref))
    env = {}

    def branch(c, k):
        cols = slice(c * MERGE_TN, (c + 1) * MERGE_TN)
        y_ref, w_ref, g_ref = branches[k]
        part = ((1.0 + jnp.tanh(g_ref[:, cols].astype(F32)))
                * jnp.dot(y_ref[...], w_ref[:, cols], preferred_element_type=F32))
        acc = part if k == 0 else env[c] + part
        if k == len(branches) - 1:
            mg_s[:, cols] = (0.5 * acc).astype(mg_s.dtype)
        else:
            env[c] = acc

    def out(c):
        cols = slice(c * MERGE_TN, (c + 1) * MERGE_TN)
        xn = x_ref[:, cols] + jnp.dot(mg_s[...], wout_ref[:, cols], preferred_element_type=F32)
        xn_s[:, cols] = xn
        if x_out_ref is not None:
            x_out_ref[:, cols] = xn
        part = jnp.sum(xn * xn, axis=-1, keepdims=True)
        ss_s[...] = part if c == 0 else ss_s[...] + part

    def norm():
        hn = xn_s[...] * lax.rsqrt(ss_s[...] * (1.0 / D_MODEL) + EPS) * ng_ref[...]
        h_ref[...] = hn.astype(h_ref.dtype)

    return ([functools.partial(branch, c, k) for c in range(n_chunks) for k in range(len(branches))]
            + [functools.partial(out, c) for c in range(n_chunks)] + [norm])


def _mixmerge_kernel(sink_ref, m_ref, kvp_ref, kvc_ref, kvn_ref,
                     cs_ref, lng_ref, lnb_ref, ws_ref, bias_ref,
                     of_ref, ob_ref, g_ref, x_ref,
                     wpa_ref, wpb_ref, wpc_ref, wout_ref, cg_ref, ng_ref,
                     *rest, n_tiles, tiles_per_seq, emit_x):
    if emit_x:
        x_out_ref, h_ref, ya_s, yb_s, yc_s, mg_s, xn_s, ss_s = rest
    else:
        x_out_ref = None
        h_ref, ya_s, yb_s, yc_s, mg_s, xn_s, ss_s = rest
    a3_ref = m_ref.at[:, 0:3 * D_A]
    q_ref = m_ref.at[:, 3 * D_A:3 * D_A + D_B]
    zb_ref = m_ref.at[:, 3 * D_A + D_B:3 * D_A + 2 * D_B]
    zc_ref = m_ref.at[:, 3 * D_A + 2 * D_B:MIX_WIDTH]
    ga_ref, gb_ref, gc_ref = [g_ref.at[:, k * D_MODEL:(k + 1) * D_MODEL] for k in range(3)]
    r = pl.program_id(0)
    slot = lax.rem(r, 2)

    @pl.when(r == 0)
    def _():
        ya_s[1] = jnp.zeros(ya_s.shape[1:], ya_s.dtype)
        yb_s[1] = jnp.zeros(yb_s.shape[1:], yb_s.dtype)
        yc_s[1] = jnp.zeros(yc_s.shape[1:], yc_s.dtype)

    pos = lax.rem(jnp.minimum(r, n_tiles - 1), tiles_per_seq)
    first = (pos == 0).astype(jnp.int32)
    final = (pos == tiles_per_seq - 1).astype(jnp.int32)
    mix_a = _mix_a_items(a3_ref, lng_ref, lnb_ref, ws_ref, bias_ref, ya_s.at[slot])
    mix_b = _mix_b_items(sink_ref, q_ref, zb_ref, kvp_ref, kvc_ref, kvn_ref, cs_ref,
                         yb_s.at[slot], first, final)
    mix_c = _mix_c_items(of_ref, ob_ref, zc_ref, cg_ref, yc_s.at[slot])
    merge = _merge_items(ya_s.at[1 - slot], yb_s.at[1 - slot], yc_s.at[1 - slot], ga_ref, gb_ref,
                         gc_ref, x_ref, wpa_ref, wpb_ref, wpc_ref, wout_ref, ng_ref, x_out_ref,
                         h_ref, mg_s, xn_s, ss_s)
    mix = mix_c[:2] + mix_b[:10] + mix_a[:5] + mix_c[2:] + mix_b[10:] + mix_a[5:]
    slots = len(merge) - 1
    done = 0
    for n, merge_item in enumerate(merge):
        merge_item()
        upto = len(mix) if n >= slots - 1 else (len(mix) * (n + 1)) // slots
        for mix_item in mix[done:upto]:
            mix_item()
        done = upto


def _mixmerge(proj, o_f, o_b, x, cs, params, layer, next_g, batch, seq, last):
    t = batch * seq
    n_tiles = t // MM_TM
    tps = seq // MM_TM
    nb2 = MM_TM // BLOCK

    def mix_tile(r):
        return jnp.minimum(r, n_tiles - 1)

    def merge_tile(r):
        return jnp.maximum(r - 1, 0)

    def kv_prev(r):
        m = mix_tile(r)
        return jnp.where(lax.rem(m, tps) == 0, nb2 * m, nb2 * m - 1)

    def kv_next(r):
        m = mix_tile(r)
        return jnp.where(lax.rem(m, tps) == tps - 1, nb2 * m + nb2 - 1, nb2 * m + nb2)

    mix = lambda width, blk: pl.BlockSpec((MM_TM, width), lambda r: (mix_tile(r), blk))
    mrg = lambda width, blk: pl.BlockSpec((MM_TM, width), lambda r: (merge_tile(r), blk))
    per_layer = lambda shape: pl.BlockSpec((None,) + shape, lambda r: (layer,) + (0,) * len(shape),
                                           pipeline_mode=pl.Buffered(1))
    weight = lambda shape: pl.BlockSpec((None,) + shape, lambda r: (layer,) + (0,) * len(shape),
                                        pipeline_mode=pl.Buffered(1))
    kvb = COL_KVB // (2 * DKV_B)
    in_specs = [
        pl.BlockSpec(memory_space=pltpu.SMEM),
        mix(MIX_WIDTH, COL_MIX // MIX_WIDTH),
        pl.BlockSpec((BLOCK, 2 * DKV_B), lambda r: (kv_prev(r), kvb)),
        mix(2 * DKV_B, kvb),
        pl.BlockSpec((BLOCK, 2 * DKV_B), lambda r: (kv_next(r), kvb)),
        pl.BlockSpec((None, MM_TM + 2 * BLOCK, 2 * HD_B),
                     lambda r: (lax.rem(mix_tile(r), tps), 0, 0)),
        per_layer((1, D_A)), per_layer((1, D_A)), per_layer((G_A, BLOCK, BLOCK)),
        per_layer((BLOCK, D_A)),
        mix(DV_C, 0), mix(DV_C, 0),
        mrg(3 * D_MODEL, COL_GATE // (3 * D_MODEL)),
        mrg(D_MODEL, 0),
        weight((D_A, D_MODEL)), weight((D_B, D_MODEL)), weight((DV_C, D_MODEL)),
        weight((D_MODEL, D_MODEL)), per_layer((1, HDV_C)),
        pl.BlockSpec((1, D_MODEL), lambda r: (0, 0)),
    ]
    h_dtype = F32 if last else BF16
    out_specs = [mrg(D_MODEL, 0)]
    out_shape = [jax.ShapeDtypeStruct((t, D_MODEL), h_dtype)]
    if not last:
        out_specs = [mrg(D_MODEL, 0)] + out_specs
        out_shape = [jax.ShapeDtypeStruct((t, D_MODEL), F32)] + out_shape
    outs = pl.pallas_call(
        functools.partial(_mixmerge_kernel, n_tiles=n_tiles, tiles_per_seq=tps, emit_x=not last),
        grid=(n_tiles + 1,),
        in_specs=in_specs,
        out_specs=out_specs,
        out_shape=out_shape,
        scratch_shapes=[pltpu.VMEM((2, MM_TM, D_A), BF16), pltpu.VMEM((2, MM_TM, D_B), BF16),
                        pltpu.VMEM((2, MM_TM, DV_C), BF16), pltpu.VMEM((MM_TM, D_MODEL), BF16),
                        pltpu.VMEM((MM_TM, D_MODEL), F32), pltpu.VMEM((MM_TM, 1), F32)],
        compiler_params=pltpu.CompilerParams(dimension_semantics=("arbitrary",),
                                             vmem_limit_bytes=VMEM_LIMIT),
        name="mixmerge",
    )(params["b_sink"][layer], proj, proj, proj, proj, cs,
      params["a_ln_g"], params["a_ln_b"], params["a_ws"], params["a_bias"],
      o_f, o_b, proj, x,
      params["w_pa"], params["w_pb"], params["w_pc"], params["w_out"], params["c_norm_g"], next_g)
    return (None, outs[0]) if last else (outs[0], outs[1])


def _rope_table(seq):
    half = HD_B // 2
    inv = ROPE_THETA ** (-jnp.arange(half, dtype=F32) * 2.0 / HD_B)
    ang = jnp.arange(seq, dtype=F32)[:, None] * inv[None, :]
    cos, sin = jnp.cos(ang), jnp.sin(ang)
    table = jnp.concatenate([cos, cos, -sin, sin], axis=-1)
    padded = jnp.pad(table, ((BLOCK, BLOCK), (0, 0)))
    tps = seq // MM_TM
    lead = padded[:seq].reshape(tps, MM_TM, 2 * HD_B)
    tail = padded[MM_TM:MM_TM + seq].reshape(tps, MM_TM, 2 * HD_B)[:, :2 * BLOCK]
    return jnp.concatenate([lead, tail], axis=1)


def _pad_gate_w(w, row0):
    z = jnp.zeros((w.shape[0], LANES, DK_C), w.dtype)
    return z.at[:, row0:row0 + GATE_RANK].set(w).astype(BF16)


def _prepare(norm_g, w_in, a_ln_g, a_ln_b, a_ws, a_bs, b_sink, c_wf, c_bf, c_wb, c_bb, c_norm_g,
             w_pa, w_pb, w_pc, w_out, final_g):
    return dict(
        norm_g=norm_g, final_g=final_g, w_in=_prep_w_in(w_in),
        a_ln_g=a_ln_g[:, None], a_ln_b=a_ln_b[:, None], a_ws=a_ws.astype(BF16),
        a_bias=jnp.repeat(jnp.swapaxes(a_bs, 1, 2), DG_A, axis=2),
        b_sink=b_sink,
        c_wf=_pad_gate_w(c_wf, 0), c_bf=c_bf, c_wb=_pad_gate_w(c_wb, GATE_RANK), c_bb=c_bb,
        c_norm_g=c_norm_g[:, None], w_pa=w_pa.astype(BF16), w_pb=w_pb.astype(BF16),
        w_pc=w_pc.astype(BF16), w_out=w_out.astype(BF16),
    )


def _trunk(x, params, batch, seq):
    t = batch * seq
    assert x.shape == (batch, seq, D_MODEL) and x.dtype == F32
    assert seq % GLA_TB == 0 and seq % MM_TM == 0 and t % PROJ_TM == 0, (batch, seq)
    x = x.reshape(t, D_MODEL)
    cs = _rope_table(seq)
    h = None
    for l in range(DEPTH):
        last = l == DEPTH - 1
        if l == 0:
            proj = _proj_norm(x, params["norm_g"][0][None], params["w_in"], l)
        else:
            proj = _proj(h, params["w_in"], l)
        o_f, o_b = _gla(proj, params["c_wf"][l], params["c_bf"][l][None], params["c_wb"][l],
                        params["c_bb"][l][None], batch, seq)
        next_g = params["final_g"] if last else params["norm_g"][l + 1]
        x, h = _mixmerge(proj, o_f, o_b, x, cs, params, l, next_g[None], batch, seq, last)
    return h.reshape(batch, seq, D_MODEL)


def kernel(x_prompt, x_sample, norm_g, w_in, a_ln_g, a_ln_b, a_ws, a_bs, b_sink, c_wf, c_bf,
           c_wb, c_bb, c_norm_g, w_pa, w_pb, w_pc, w_out, final_g):
    params = _prepare(norm_g, w_in, a_ln_g, a_ln_b, a_ws, a_bs, b_sink, c_wf, c_bf, c_wb, c_bb,
                      c_norm_g, w_pa, w_pb, w_pc, w_out, final_g)
    y_prompt = _trunk(x_prompt, params, x_prompt.shape[0], x_prompt.shape[1])
    y_sample = _trunk(x_sample, params, x_sample.shape[0], x_sample.shape[1])
    return (y_prompt, y_sample)
```
